```python
import jax
import jax.numpy as jnp
from jax import lax
import numpy as np

D_MODEL = 1024
BATCH = 16
SEQ = 4096
DEPTH = 4

N_EVEN = (DEPTH + 1) // 2
N_ODD = DEPTH // 2
EPS = 1e-6
ROPE_THETA = 10000.0

D_FF = -(-(8 * D_MODEL) // (3 * 256)) * 256

CONV_WIDTH = D_MODEL // 2
POOL_WIDTH = D_MODEL - CONV_WIDTH
POOL_WINDOWS = (2, 4, 8, 16)
POOL_GROUP = POOL_WIDTH // len(POOL_WINDOWS)
SHORT_CONV_K = 3
EVEN_IN = 3 * CONV_WIDTH + POOL_WIDTH

HEAD_DIM = 64
NSA_WIDTH = D_MODEL // 2
NSA_HEADS = NSA_WIDTH // HEAD_DIM
NSA_KV_GROUPS = 2
GROUP_SIZE = NSA_HEADS // NSA_KV_GROUPS
KV_WIDTH = NSA_KV_GROUPS * HEAD_DIM
CMP_BLOCK = 32
CMP_STRIDE = 16
CMP_HIDDEN = 4 * HEAD_DIM
SEL_BLOCK = 64
N_SELECT = 8
WINDOW = 512
Q_BLOCK = 128
FORCE_SCORE = 1e4
NEG = -1e30
TINY = 1e-30
CONF_WIDTH = D_MODEL - NSA_WIDTH
CONF_K = 31
ODD_PARTS = (NSA_WIDTH, KV_WIDTH, KV_WIDTH, KV_WIDTH, KV_WIDTH, KV_WIDTH, KV_WIDTH,
             3 * NSA_HEADS, 2 * CONF_WIDTH)
ODD_IN = sum(ODD_PARTS)

kernel_name = 'hybrid_shortconv_pool_nsa_conformer_trunk'


def rms_norm(x, g):
    xf = x.astype(jnp.float32)
    y = xf * lax.rsqrt(jnp.mean(xf * xf, axis=-1, keepdims=True) + EPS)
    return (y * g.astype(jnp.float32)).astype(x.dtype)


def layer_norm(x, g, b):
    xf = x.astype(jnp.float32)
    mu = jnp.mean(xf, axis=-1, keepdims=True)
    var = jnp.mean(jnp.square(xf - mu), axis=-1, keepdims=True)
    y = (xf - mu) * lax.rsqrt(var + EPS)
    return (y * g.astype(jnp.float32) + b.astype(jnp.float32)).astype(x.dtype)


def rope_tables(seq_len):
    inv = 1.0 / (ROPE_THETA ** (jnp.arange(0, HEAD_DIM, 2, dtype=jnp.float32) / HEAD_DIM))
    ang = jnp.arange(seq_len, dtype=jnp.float32)[:, None] * inv[None, :]
    return jnp.cos(ang), jnp.sin(ang)


def apply_rope(x, cos, sin):
    xf = x.astype(jnp.float32)
    x1, x2 = jnp.split(xf, 2, axis=-1)
    c = cos[None, :, None, :]
    s = sin[None, :, None, :]
    return jnp.concatenate([x1 * c - x2 * s, x2 * c + x1 * s], axis=-1).astype(x.dtype)


def causal_dwconv(x, w):
    k, c = w.shape
    return lax.conv_general_dilated(
        x, w[:, None, :].astype(x.dtype), window_strides=(1,), padding=[(k - 1, 0)],
        dimension_numbers=('NWC', 'WIO', 'NWC'), feature_group_count=c)


def masked_softmax(s, mask):
    s = jnp.where(mask, s.astype(jnp.float32), NEG)
    m = jnp.max(s, axis=-1, keepdims=True)
    e = jnp.where(mask, jnp.exp(s - m), 0.0)
    return e / jnp.maximum(jnp.sum(e, axis=-1, keepdims=True), TINY)


def swiglu(h, wg, wu, wd):
    return (jax.nn.silu(h @ wg) * (h @ wu)) @ wd


def multiscale_pool(v, pool_w, pool_scale):
    bsz, t, _ = v.shape
    vg = v.reshape(bsz, t, len(POOL_WINDOWS), POOL_GROUP)
    cs = jnp.cumsum(vg.astype(jnp.float32), axis=1)
    pos = jnp.arange(t)
    means = []
    for gi, w in enumerate(POOL_WINDOWS):
        c = cs[:, :, gi]
        prev = jnp.pad(c, ((0, 0), (w, 0), (0, 0)))[:, :t]
        cnt = jnp.minimum(pos + 1, w).astype(jnp.float32)[None, :, None]
        means.append((c - prev) / cnt)
    pooled = jnp.stack(means, axis=2).astype(v.dtype) - vg
    y = jnp.einsum('btgc,gcd->btgd', pooled, pool_w)
    return y.reshape(bsz, t, POOL_WIDTH) * pool_scale


def short_conv_pool_mixer(h, w_in, conv_w, pool_w, pool_scale, w_out):
    u = h @ w_in
    b_gate, c_gate, v_conv, v_pool = jnp.split(
        u, [CONV_WIDTH, 2 * CONV_WIDTH, 3 * CONV_WIDTH], axis=-1)
    y_conv = b_gate * causal_dwconv(c_gate * v_conv, conv_w)
    y_pool = multiscale_pool(v_pool, pool_w, pool_scale)
    return jnp.concatenate([y_conv, y_pool], axis=-1) @ w_out


def compress_blocks(tok, pos_emb, w1, w2):
    bsz, g, t, dh = tok.shape
    n_cmp = (t - CMP_BLOCK) // CMP_STRIDE + 1
    idx = jnp.arange(n_cmp)[:, None] * CMP_STRIDE + jnp.arange(CMP_BLOCK)[None, :]
    blocks = tok[:, :, idx] + pos_emb
    flat = blocks.reshape(bsz, g, n_cmp, CMP_BLOCK * dh)
    return jax.nn.gelu(flat @ w1) @ w2


def native_sparse_attention(q, k_cmp, v_cmp, k_slc, v_slc, k_win, v_win, gates,
                            q_norm, k_norm, cmp_pos, cmp_w1, cmp_w2, cos, sin):
    bsz, t = q.shape[0], q.shape[1]
    q = apply_rope(rms_norm(q, q_norm), cos, sin)
    ks = apply_rope(rms_norm(k_slc, k_norm[0]), cos, sin)
    kw = apply_rope(rms_norm(k_win, k_norm[1]), cos, sin)
    kc_tok = apply_rope(k_cmp, cos, sin)
    to_bgtd = lambda a: a.transpose(0, 2, 1, 3)
    kc = rms_norm(compress_blocks(to_bgtd(kc_tok), cmp_pos[0], cmp_w1[0], cmp_w2[0]), k_norm[2])
    vc = compress_blocks(to_bgtd(v_cmp), cmp_pos[1], cmp_w1[1], cmp_w2[1])
    ks, vs, kw, vw = to_bgtd(ks), to_bgtd(v_slc), to_bgtd(kw), to_bgtd(v_win)

    q5 = q.reshape(bsz, t, NSA_KV_GROUPS, GROUP_SIZE, HEAD_DIM).transpose(0, 2, 3, 1, 4)
    g5 = gates.reshape(bsz, t, NSA_KV_GROUPS, GROUP_SIZE, 3).transpose(0, 2, 3, 1, 4)

    n_cmp = kc.shape[2]
    n_sb = t // SEL_BLOCK
    n_sel = min(N_SELECT, n_sb)
    cmp_start = jnp.arange(n_cmp) * CMP_STRIDE
    cmp_end = cmp_start + CMP_BLOCK - 1
    sb_start = jnp.arange(n_sb) * SEL_BLOCK
    overlap = ((cmp_start[:, None] < sb_start[None, :] + SEL_BLOCK)
               & (cmp_end[:, None] >= sb_start[None, :])).astype(jnp.float32)
    ks_blocks = ks.reshape(bsz, NSA_KV_GROUPS, n_sb, SEL_BLOCK, HEAD_DIM)
    vs_blocks = vs.reshape(bsz, NSA_KV_GROUPS, n_sb, SEL_BLOCK, HEAD_DIM)
    kw_pad = jnp.pad(kw, ((0, 0), (0, 0), (WINDOW, 0), (0, 0)))
    vw_pad = jnp.pad(vw, ((0, 0), (0, 0), (WINDOW, 0), (0, 0)))
    bi = jnp.arange(bsz)[:, None, None, None]
    gi = jnp.arange(NSA_KV_GROUPS)[None, :, None, None]
    scale = HEAD_DIM ** -0.5

    def query_block(qb):
        q0 = qb * Q_BLOCK
        qblk = lax.dynamic_slice_in_dim(q5, q0, Q_BLOCK, axis=3)
        gblk = lax.dynamic_slice_in_dim(g5, q0, Q_BLOCK, axis=3)
        pos = q0 + jnp.arange(Q_BLOCK)
        s_c = jnp.einsum('bgrqd,bgnd->bgrqn', qblk, kc).astype(jnp.float32) * scale
        p_cmp = masked_softmax(s_c, cmp_end[None, :] <= pos[:, None])
        o_cmp = jnp.einsum('bgrqn,bgnd->bgrqd', p_cmp.astype(vc.dtype), vc)
        imp = jnp.einsum('bgrqn,nj->bgqj', p_cmp, overlap)
        cur = pos // SEL_BLOCK
        jj = jnp.arange(n_sb)[None, :]
        forced = (jj == 0) | (jj == cur[:, None]) | (jj == cur[:, None] - 1)
        valid = sb_start[None, :] <= pos[:, None]
        score = jnp.where(valid, jnp.where(forced, FORCE_SCORE, imp), -1.0)
        _, top_idx = lax.top_k(score, n_sel)
        k_sel = ks_blocks[bi, gi, top_idx].reshape(bsz, NSA_KV_GROUPS, Q_BLOCK, n_sel * SEL_BLOCK, HEAD_DIM)
        v_sel = vs_blocks[bi, gi, top_idx].reshape(bsz, NSA_KV_GROUPS, Q_BLOCK, n_sel * SEL_BLOCK, HEAD_DIM)
        tok_pos = (top_idx[..., None] * SEL_BLOCK + jnp.arange(SEL_BLOCK)).reshape(
            bsz, NSA_KV_GROUPS, Q_BLOCK, n_sel * SEL_BLOCK)
        sel_mask = (tok_pos <= pos[None, None, :, None])[:, :, None]
        s_s = jnp.einsum('bgrqd,bgqkd->bgrqk', qblk, k_sel).astype(jnp.float32) * scale
        p_s = masked_softmax(s_s, sel_mask)
        o_slc = jnp.einsum('bgrqk,bgqkd->bgrqd', p_s.astype(v_sel.dtype), v_sel)
        k_band = lax.dynamic_slice_in_dim(kw_pad, q0, WINDOW + Q_BLOCK, axis=2)
        v_band = lax.dynamic_slice_in_dim(vw_pad, q0, WINDOW + Q_BLOCK, axis=2)
        kpos = q0 - WINDOW + jnp.arange(WINDOW + Q_BLOCK)
        w_mask = ((kpos[None, :] <= pos[:, None]) & (kpos[None, :] > pos[:, None] - WINDOW)
                  & (kpos[None, :] >= 0))
        s_w = jnp.einsum('bgrqd,bgkd->bgrqk', qblk, k_band).astype(jnp.float32) * scale
        p_w = masked_softmax(s_w, w_mask)
        o_win = jnp.einsum('bgrqk,bgkd->bgrqd', p_w.astype(v_band.dtype), v_band)
        return gblk[..., 0:1] * o_cmp + gblk[..., 1:2] * o_slc + gblk[..., 2:3] * o_win

    out = lax.map(query_block, jnp.arange(t // Q_BLOCK))
    return out.transpose(1, 0, 4, 2, 3, 5).reshape(bsz, t, NSA_WIDTH)


def sparse_attn_conformer_mixer(h, w_in, q_norm, k_norm, cmp_pos, cmp_w1, cmp_w2,
                                conf_dw, conf_dw_b, conf_ln_g, conf_ln_b, w_out, cos, sin):
    bsz, t, _ = h.shape
    u = h @ w_in
    offs = []
    acc = 0
    for p in ODD_PARTS[:-1]:
        acc += p
        offs.append(acc)
    q, kc, vc, ks, vs, kw, vw, gt, cf = jnp.split(u, offs, axis=-1)
    heads = lambda a: a.reshape(bsz, t, NSA_HEADS, HEAD_DIM)
    kvs = lambda a: a.reshape(bsz, t, NSA_KV_GROUPS, HEAD_DIM)
    gates = jax.nn.sigmoid(gt).reshape(bsz, t, NSA_HEADS, 3)
    o_nsa = native_sparse_attention(heads(q), kvs(kc), kvs(vc), kvs(ks), kvs(vs), kvs(kw), kvs(vw),
                                    gates, q_norm, k_norm, cmp_pos, cmp_w1, cmp_w2, cos, sin)
    a, b = jnp.split(cf, 2, axis=-1)
    z = a * jax.nn.sigmoid(b)
    z = causal_dwconv(z, conf_dw) + conf_dw_b
    z = jax.nn.silu(layer_norm(z, conf_ln_g, conf_ln_b))
    return jnp.concatenate([o_nsa, z], axis=-1) @ w_out


def setup_inputs(seed: int = 0) -> dict:
    key = jax.random.key(seed)
    ks = jax.random.split(key, 24)
    nrm = lambda k, shape, s: jax.random.normal(k, shape, jnp.float32) * s
    d = D_MODEL
    return {
        'x': nrm(ks[0], (BATCH, SEQ, d), 1.0),
        'norm_mix': 1.0 + nrm(ks[1], (DEPTH, d), 0.05),
        'norm_ffn': 1.0 + nrm(ks[2], (DEPTH, d), 0.05),
        'ffn_gate': nrm(ks[3], (DEPTH, d, D_FF), d ** -0.5),
        'ffn_up': nrm(ks[4], (DEPTH, d, D_FF), d ** -0.5),
        'ffn_down': nrm(ks[5], (DEPTH, D_FF, d), 0.5 * D_FF ** -0.5),
        'w_in_even': nrm(ks[6], (N_EVEN, d, EVEN_IN), d ** -0.5),
        'conv_a': nrm(ks[7], (N_EVEN, SHORT_CONV_K, CONV_WIDTH), SHORT_CONV_K ** -0.5),
        'pool_w': nrm(ks[8], (N_EVEN, len(POOL_WINDOWS), POOL_GROUP, POOL_GROUP), POOL_GROUP ** -0.5),
        'pool_scale': 1.0 + nrm(ks[9], (N_EVEN, POOL_WIDTH), 0.1),
        'w_out_even': nrm(ks[10], (N_EVEN, d, d), 0.5 * d ** -0.5),
        'w_in_odd': nrm(ks[11], (N_ODD, d, ODD_IN), d ** -0.5),
        'q_norm': 1.0 + nrm(ks[12], (N_ODD, HEAD_DIM), 0.05),
        'k_norm': 1.0 + nrm(ks[13], (N_ODD, 3, HEAD_DIM), 0.05),
        'cmp_pos': nrm(ks[14], (N_ODD, 2, CMP_BLOCK, HEAD_DIM), 0.1),
        'cmp_w1': nrm(ks[15], (N_ODD, 2, CMP_BLOCK * HEAD_DIM, CMP_HIDDEN), (CMP_BLOCK * HEAD_DIM) ** -0.5),
        'cmp_w2': nrm(ks[16], (N_ODD, 2, CMP_HIDDEN, HEAD_DIM), CMP_HIDDEN ** -0.5),
        'conf_dw': nrm(ks[17], (N_ODD, CONF_K, CONF_WIDTH), CONF_K ** -0.5),
        'conf_dw_b': nrm(ks[18], (N_ODD, CONF_WIDTH), 0.02),
        'conf_ln_g': 1.0 + nrm(ks[19], (N_ODD, CONF_WIDTH), 0.05),
        'conf_ln_b': nrm(ks[20], (N_ODD, CONF_WIDTH), 0.02),
        'w_out_odd': nrm(ks[21], (N_ODD, d, d), 0.5 * d ** -0.5),
    }


def reference(x, norm_mix, norm_ffn, ffn_gate, ffn_up, ffn_down,
              w_in_even, conv_a, pool_w, pool_scale, w_out_even,
              w_in_odd, q_norm, k_norm, cmp_pos, cmp_w1, cmp_w2,
              conf_dw, conf_dw_b, conf_ln_g, conf_ln_b, w_out_odd):
    cos, sin = rope_tables(x.shape[1])
    for i in range(DEPTH):
        h = rms_norm(x, norm_mix[i])
        if i % 2 == 0:
            e = i // 2
            x = x + short_conv_pool_mixer(h, w_in_even[e], conv_a[e], pool_w[e],
                                          pool_scale[e], w_out_even[e])
        else:
            o = i // 2
            x = x + sparse_attn_conformer_mixer(h, w_in_odd[o], q_norm[o], k_norm[o], cmp_pos[o],
                                                cmp_w1[o], cmp_w2[o], conf_dw[o], conf_dw_b[o],
                                                conf_ln_g[o], conf_ln_b[o], w_out_odd[o], cos, sin)
        x = x + swiglu(rms_norm(x, norm_ffn[i]), ffn_gate[i], ffn_up[i], ffn_down[i])
    return x
```

```python
import functools

import numpy as np
import jax
import jax.numpy as jnp
from jax import lax
from jax.experimental import pallas as pl
from jax.experimental.pallas import tpu as pltpu

EPS = 1e-6
ROPE_THETA = 10000.0
HEAD_DIM = 64
HALF = HEAD_DIM // 2
LANES = 128
N_HEADS = 8
N_GROUPS = 2
GROUP_SIZE = N_HEADS // N_GROUPS
NSA_WIDTH = N_HEADS * HEAD_DIM
KV_WIDTH = N_GROUPS * HEAD_DIM
CONF_WIDTH = 512
CONF_K = 31
CONV_WIDTH = 512
POOL_WINDOWS = (2, 4, 8, 16)
POOL_GROUP = 128
SHORT_CONV_K = 3
CMP_BLOCK = 32
CMP_STRIDE = 16
CMP_HIDDEN = 4 * HEAD_DIM
SEL_BLOCK = 64
N_SELECT = 8
WINDOW = 512
Q_BLOCK = 128
FORCE_SCORE = 1e4
NEG = -1e30
TINY = 1e-30
GATE_PAD = LANES
ODD_COLS = NSA_WIDTH + 6 * KV_WIDTH + GATE_PAD + 2 * CONF_WIDTH

VMEM_LIMIT = 56 * 1024 * 1024

BF16 = jnp.bfloat16
F32 = jnp.float32


def _dot(a, b):
    return jnp.dot(a, b, preferred_element_type=F32)


def _dot_nt(a, b):
    return lax.dot_general(a, b, (((1,), (1,)), ((), ())), preferred_element_type=F32)


def _split_dot(a, b):
    hi = a.astype(BF16)
    lo = (a - hi.astype(F32)).astype(BF16)
    return _dot(hi, b) + _dot(lo, b)


def _rms_rows(x, g):
    ms = jnp.mean(x * x, axis=-1, keepdims=True)
    return x * lax.rsqrt(ms + EPS) * g


def _sigmoid(x):
    return 1.0 / (1.0 + jnp.exp(-x))


def _const_spec(shape):
    nd = len(shape)
    return pl.BlockSpec(shape, lambda *_: (0,) * nd, pipeline_mode=pl.Buffered(1))


def _ffn_kernel(x_ref, g_ref, wg_ref, wu_ref, wd_ref, o_ref, *, n_chunks):
    x = x_ref[0]
    h = _rms_rows(x, g_ref[...]).astype(BF16)
    acc = x
    for c in range(n_chunks):
        gate = _dot(h, wg_ref[c])
        up = _dot(h, wu_ref[c])
        a = (gate * _sigmoid(gate) * up).astype(BF16)
        acc = acc + _dot(a, wd_ref[c])
    o_ref[0] = acc


def _ffn(x, g, wg, wu, wd, *, tm, chunk):
    b, t, d = x.shape
    dff = wg.shape[1]
    n_chunks = dff // chunk
    wg_c = wg.astype(BF16).reshape(d, n_chunks, chunk).transpose(1, 0, 2)
    wu_c = wu.astype(BF16).reshape(d, n_chunks, chunk).transpose(1, 0, 2)
    wd_c = wd.astype(BF16).reshape(n_chunks, chunk, d)
    return pl.pallas_call(
        functools.partial(_ffn_kernel, n_chunks=n_chunks),
        grid=(b, t // tm),
        in_specs=[
            pl.BlockSpec((1, tm, d), lambda i, j: (i, j, 0)),
            _const_spec((1, d)),
            _const_spec((n_chunks, d, chunk)),
            _const_spec((n_chunks, d, chunk)),
            _const_spec((n_chunks, chunk, d)),
        ],
        out_specs=pl.BlockSpec((1, tm, d), lambda i, j: (i, j, 0)),
        out_shape=jax.ShapeDtypeStruct(x.shape, F32),
        compiler_params=pltpu.CompilerParams(
            dimension_semantics=("arbitrary", "arbitrary"), vmem_limit_bytes=VMEM_LIMIT),
        name="ffn",
    )(x, g.reshape(1, d), wg_c, wu_c, wd_c)


def _even_kernel(x_ref, g_ref, win_ref, cw_ref, pw_ref, ps_ref, wout_ref, o_ref,
                 cv_buf, vp_buf, *, tm):
    t = pl.program_id(1)
    x = x_ref[0]
    h = _rms_rows(x, g_ref[...]).astype(BF16)
    u = _dot(h, win_ref[...])
    b_gate = u[:, 0:CONV_WIDTH]
    cv = u[:, CONV_WIDTH:2 * CONV_WIDTH] * u[:, 2 * CONV_WIDTH:3 * CONV_WIDTH]
    vp = u[:, 3 * CONV_WIDTH:]

    @pl.when(t == 0)
    def _():
        cv_buf[0:8, :] = jnp.zeros((8, CONV_WIDTH), F32)
        vp_buf[0:16, :] = jnp.zeros((16, 4 * POOL_GROUP), F32)

    @pl.when(t > 0)
    def _():
        cv_buf[0:8, :] = cv_buf[tm:tm + 8, :]
        vp_buf[0:16, :] = vp_buf[tm:tm + 16, :]

    cv_buf[8:8 + tm, :] = cv
    vp_buf[16:16 + tm, :] = vp

    cw = cw_ref[...]
    conv = cw[2:3, :] * cv
    for k in range(SHORT_CONV_K - 1):
        off = 8 - (SHORT_CONV_K - 1) + k
        conv = conv + cw[k:k + 1, :] * cv_buf[off:off + tm, :]
    y_conv = b_gate * conv

    pos = t * tm + lax.broadcasted_iota(jnp.int32, (tm, 1), 0)
    ys = []
    for gi, w in enumerate(POOL_WINDOWS):
        lo, hi = gi * POOL_GROUP, (gi + 1) * POOL_GROUP
        s = vp[:, lo:hi]
        for k in range(1, w):
            s = s + vp_buf[16 - k:16 - k + tm, lo:hi]
        cnt = jnp.minimum(pos + 1, w).astype(F32)
        pooled = s / cnt - vp[:, lo:hi]
        ys.append(_dot(pooled.astype(BF16), pw_ref[gi]))
    y_pool = jnp.concatenate(ys, axis=-1) * ps_ref[...]
    y = jnp.concatenate([y_conv, y_pool], axis=-1).astype(BF16)
    o_ref[0] = x + _dot(y, wout_ref[...])


def _even_mixer(x, g, w_in, conv_w, pool_w, pool_scale, w_out, *, tm):
    b, t, d = x.shape
    n_in = w_in.shape[1]
    return pl.pallas_call(
        functools.partial(_even_kernel, tm=tm),
        grid=(b, t // tm),
        in_specs=[
            pl.BlockSpec((1, tm, d), lambda i, j: (i, j, 0)),
            _const_spec((1, d)),
            _const_spec((d, n_in)),
            _const_spec((SHORT_CONV_K, CONV_WIDTH)),
            _const_spec((len(POOL_WINDOWS), POOL_GROUP, POOL_GROUP)),
            _const_spec((1, 4 * POOL_GROUP)),
            _const_spec((d, d)),
        ],
        out_specs=pl.BlockSpec((1, tm, d), lambda i, j: (i, j, 0)),
        out_shape=jax.ShapeDtypeStruct(x.shape, F32),
        scratch_shapes=[pltpu.VMEM((tm + 8, CONV_WIDTH), F32),
                        pltpu.VMEM((tm + 16, 4 * POOL_GROUP), F32)],
        compiler_params=pltpu.CompilerParams(
            dimension_semantics=("arbitrary", "arbitrary"), vmem_limit_bytes=VMEM_LIMIT),
        name="even_mixer",
    )(x, g.reshape(1, d), w_in.astype(BF16), conv_w, pool_w.astype(BF16),
      pool_scale.reshape(1, -1), w_out.astype(BF16))


def _rope_tile(x, cos, sin_signed, first_half):
    rot = jnp.where(first_half, pltpu.roll(x, LANES - HALF, 1), pltpu.roll(x, HALF, 1))
    return x * cos + rot * sin_signed


def _head_ms(x, ones_bd):
    return _split_dot(x * x, ones_bd) * (1.0 / HEAD_DIM)


def _odd_proj_kernel(x_ref, g_ref, win_ref, cos_ref, sin_ref, qn_ref, kn_ref, bd_ref,
                     dw_ref, dwb_ref, lng_ref, lnb_ref,
                     q_ref, kc_ref, vc_ref, ks_ref, vs_ref, kw_ref, vw_ref, gt_ref, z_ref,
                     z_buf, *, tm):
    t = pl.program_id(1)
    x = x_ref[0]
    h = _rms_rows(x, g_ref[...]).astype(BF16)
    u = _dot(h, win_ref[...])
    cos = cos_ref[...]
    sin = sin_ref[...]
    lane = lax.broadcasted_iota(jnp.int32, (1, LANES), 1)
    first_half = (lane % HEAD_DIM) < HALF
    bd = bd_ref[...]

    scale = HEAD_DIM ** -0.5
    q_parts = []
    for j in range(NSA_WIDTH // LANES):
        qj = u[:, j * LANES:(j + 1) * LANES]
        qj = qj * lax.rsqrt(_head_ms(qj, bd) + EPS) * qn_ref[...]
        q_parts.append(_rope_tile(qj, cos, sin, first_half) * scale)
    q_ref[0] = jnp.concatenate(q_parts, axis=-1).astype(BF16)

    o = NSA_WIDTH
    kc_ref[0] = _rope_tile(u[:, o:o + KV_WIDTH], cos, sin, first_half).astype(BF16)
    o += KV_WIDTH
    vc_ref[0] = u[:, o:o + KV_WIDTH].astype(BF16)
    o += KV_WIDTH
    k = u[:, o:o + KV_WIDTH]
    k = k * lax.rsqrt(_head_ms(k, bd) + EPS) * kn_ref[0:1, :]
    ks_ref[0] = _rope_tile(k, cos, sin, first_half).astype(BF16)
    o += KV_WIDTH
    vs_ref[0] = u[:, o:o + KV_WIDTH].astype(BF16)
    o += KV_WIDTH
    k = u[:, o:o + KV_WIDTH]
    k = k * lax.rsqrt(_head_ms(k, bd) + EPS) * kn_ref[1:2, :]
    kw_ref[0] = _rope_tile(k, cos, sin, first_half).astype(BF16)
    o += KV_WIDTH
    vw_ref[0] = u[:, o:o + KV_WIDTH].astype(BF16)
    o += KV_WIDTH
    gt_ref[0] = _sigmoid(u[:, o:o + GATE_PAD])
    o += GATE_PAD

    zz = u[:, o:o + CONF_WIDTH] * _sigmoid(u[:, o + CONF_WIDTH:o + 2 * CONF_WIDTH])
    halo = 32

    @pl.when(t == 0)
    def _():
        z_buf[0:halo, :] = jnp.zeros((halo, CONF_WIDTH), F32)

    @pl.when(t > 0)
    def _():
        z_buf[0:halo, :] = z_buf[tm:tm + halo, :]

    z_buf[halo:halo + tm, :] = zz
    dw = dw_ref[...]
    conv = dw[CONF_K - 1:CONF_K, :] * zz + dwb_ref[...]
    for kk in range(CONF_K - 1):
        off = halo - (CONF_K - 1) + kk
        conv = conv + dw[kk:kk + 1, :] * z_buf[off:off + tm, :]
    mu = jnp.mean(conv, axis=-1, keepdims=True)
    cen = conv - mu
    var = jnp.mean(cen * cen, axis=-1, keepdims=True)
    y = cen * lax.rsqrt(var + EPS) * lng_ref[...] + lnb_ref[...]
    z_ref[0] = (y * _sigmoid(y)).astype(BF16)


def _q_perm():
    p = np.arange(NSA_WIDTH)
    j, within = p // LANES, p % LANES
    head = np.where(within < HEAD_DIM, j, GROUP_SIZE + j)
    return head * HEAD_DIM + within % HEAD_DIM


def _odd_proj(x, g, w_in, q_norm, k_norm, conf_dw, conf_dw_b, conf_ln_g, conf_ln_b, *, tm):
    b, t, d = x.shape
    perm = _q_perm()
    n_kv = 6 * KV_WIDTH
    n_gate = 3 * N_HEADS
    w_q = w_in[:, :NSA_WIDTH][:, perm]
    w_kv = w_in[:, NSA_WIDTH:NSA_WIDTH + n_kv]
    w_gt = jnp.pad(w_in[:, NSA_WIDTH + n_kv:NSA_WIDTH + n_kv + n_gate], ((0, 0), (0, GATE_PAD - n_gate)))
    w_cf = w_in[:, NSA_WIDTH + n_kv + n_gate:]
    w_all = jnp.concatenate([w_q, w_kv, w_gt, w_cf], axis=1).astype(BF16)

    inv = 1.0 / (ROPE_THETA ** (jnp.arange(0, HEAD_DIM, 2, dtype=F32) / HEAD_DIM))
    ang = jnp.arange(t, dtype=F32)[:, None] * inv[None, :]
    cos = jnp.tile(jnp.cos(ang), (1, LANES // HALF))
    sin = jnp.tile(jnp.sin(ang), (1, LANES // HALF))
    lane = np.arange(LANES)
    sin_signed = jnp.where((lane % HEAD_DIM < HALF)[None, :], -sin, sin)

    ones_bd = jnp.asarray((lane[:, None] // HEAD_DIM == lane[None, :] // HEAD_DIM), BF16)
    qn = jnp.tile(q_norm, LANES // HEAD_DIM).reshape(1, LANES)
    kn = jnp.tile(k_norm[:2], (1, LANES // HEAD_DIM))

    tok = lambda w, dt: jax.ShapeDtypeStruct((b, t, w), dt)
    tok_spec = lambda w: pl.BlockSpec((1, tm, w), lambda i, j: (i, j, 0))
    return pl.pallas_call(
        functools.partial(_odd_proj_kernel, tm=tm),
        grid=(b, t // tm),
        in_specs=[
            tok_spec(d),
            _const_spec((1, d)),
            _const_spec((d, ODD_COLS)),
            pl.BlockSpec((tm, LANES), lambda i, j: (j, 0)),
            pl.BlockSpec((tm, LANES), lambda i, j: (j, 0)),
            _const_spec((1, LANES)),
            _const_spec((2, LANES)),
            _const_spec((LANES, LANES)),
            _const_spec((CONF_K, CONF_WIDTH)),
            _const_spec((1, CONF_WIDTH)),
            _const_spec((1, CONF_WIDTH)),
            _const_spec((1, CONF_WIDTH)),
        ],
        out_specs=[tok_spec(NSA_WIDTH)] + [tok_spec(KV_WIDTH)] * 6 + [tok_spec(GATE_PAD), tok_spec(CONF_WIDTH)],
        out_shape=[tok(NSA_WIDTH, BF16)] + [tok(KV_WIDTH, BF16)] * 6 + [tok(GATE_PAD, F32), tok(CONF_WIDTH, BF16)],
        scratch_shapes=[pltpu.VMEM((tm + 32, CONF_WIDTH), F32)],
        compiler_params=pltpu.CompilerParams(
            dimension_semantics=("arbitrary", "arbitrary"), vmem_limit_bytes=VMEM_LIMIT),
        name="odd_proj",
    )(x, g.reshape(1, d), w_all, cos, sin_signed, qn, kn, ones_bd,
      conf_dw, conf_dw_b.reshape(1, -1), conf_ln_g.reshape(1, -1), conf_ln_b.reshape(1, -1))


def _gelu_tanh(x):
    return 0.5 * x * (1.0 + jnp.tanh(np.sqrt(2.0 / np.pi) * (x + 0.044715 * (x * x * x))))


def _compress_kernel(kt_ref, vt_ref, w1_ref, pos_ref, w2_ref, kn_ref, kc_ref, vc_ref, sh_buf, *, n_rows):
    hw = N_GROUPS * CMP_HIDDEN
    outs = []
    for s, tok_ref in enumerate((kt_ref, vt_ref)):
        rows = tok_ref[0]
        first = _dot(rows, w1_ref[s, 0])
        second = _dot(rows, w1_ref[s, 1])
        sh_buf[0:n_rows, :] = second
        sh_buf[n_rows:n_rows + 8, :] = jnp.zeros((8, hw), F32)
        bias = _dot(pos_ref[s, 0], w1_ref[s, 0]) + _dot(pos_ref[s, 1], w1_ref[s, 1])
        hid = first + sh_buf[1:n_rows + 1, :] + bias[0:1, :]
        act = _gelu_tanh(hid).astype(BF16)
        per_group = [_dot(act[:, gi * CMP_HIDDEN:(gi + 1) * CMP_HIDDEN], w2_ref[s]) for gi in range(N_GROUPS)]
        outs.append(per_group)
    kc = [_rms_rows(o, kn_ref[...]) for o in outs[0]]
    kc_ref[0] = jnp.concatenate(kc, axis=-1).astype(BF16)
    vc_ref[0] = jnp.concatenate(outs[1], axis=-1).astype(BF16)


def _compress(kc_tok, vc_tok, cmp_pos, cmp_w1, cmp_w2, k_norm2):
    b, t, _ = kc_tok.shape
    n_rows = t // CMP_STRIDE
    row_w = CMP_STRIDE * KV_WIDTH
    hw = N_GROUPS * CMP_HIDDEN
    w1 = cmp_w1.reshape(2, 2, CMP_STRIDE, HEAD_DIM, CMP_HIDDEN)
    eye = jnp.eye(N_GROUPS, dtype=cmp_w1.dtype)
    w1x = jnp.einsum('shrdc,gk->shrgdkc', w1, eye).reshape(2, 2, row_w, hw).astype(BF16)
    pos = cmp_pos.reshape(2, 2, CMP_STRIDE, 1, HEAD_DIM)
    posx = jnp.broadcast_to(pos, (2, 2, CMP_STRIDE, N_GROUPS, HEAD_DIM)).reshape(2, 2, 1, row_w)
    posx = jnp.broadcast_to(posx, (2, 2, 8, row_w)).astype(BF16)
    out = jax.ShapeDtypeStruct((b, n_rows, KV_WIDTH), BF16)
    return pl.pallas_call(
        functools.partial(_compress_kernel, n_rows=n_rows),
        grid=(b,),
        in_specs=[
            pl.BlockSpec((1, n_rows, row_w), lambda i: (i, 0, 0)),
            pl.BlockSpec((1, n_rows, row_w), lambda i: (i, 0, 0)),
            _const_spec((2, 2, row_w, hw)),
            _const_spec((2, 2, 8, row_w)),
            _const_spec((2, CMP_HIDDEN, HEAD_DIM)),
            _const_spec((1, HEAD_DIM)),
        ],
        out_specs=[pl.BlockSpec((1, n_rows, KV_WIDTH), lambda i: (i, 0, 0))] * 2,
        out_shape=[out, out],
        scratch_shapes=[pltpu.VMEM((n_rows + 8, hw), F32)],
        compiler_params=pltpu.CompilerParams(
            dimension_semantics=("arbitrary",), vmem_limit_bytes=VMEM_LIMIT),
        name="compress",
    )(kc_tok.reshape(b, n_rows, row_w), vc_tok.reshape(b, n_rows, row_w), w1x, posx,
      cmp_w2.astype(BF16), k_norm2.reshape(1, HEAD_DIM))


def _softmax_rows(s, mask):
    s = jnp.where(mask, s, NEG)
    m = jnp.max(s, axis=-1, keepdims=True)
    e = jnp.where(mask, jnp.exp(s - m), 0.0)
    return e / jnp.maximum(jnp.sum(e, axis=-1, keepdims=True), TINY)


def _nsa_kernel(q_ref, gt_ref, kc_ref, vc_ref, ks_ref, vs_ref, kw_ref, vw_ref, ov_ref, ge_ref, o_ref,
                *, t_len, ck):
    qb = pl.program_id(1)
    q0 = qb * Q_BLOCK
    rows = GROUP_SIZE * Q_BLOCK
    n_cmp_pad = kc_ref.shape[1]
    n_sb = t_len // SEL_BLOCK
    n_sel = min(N_SELECT, n_sb)
    win_len = min(WINDOW + Q_BLOCK, t_len)

    q_all = q_ref[0]
    lane = lax.broadcasted_iota(jnp.int32, (1, LANES), 1)
    pos_r = q0 + (lax.broadcasted_iota(jnp.int32, (rows, 1), 0) % Q_BLOCK)
    pos_q = q0 + lax.broadcasted_iota(jnp.int32, (Q_BLOCK, 1), 0)

    branch_out = [[], [], []]
    for g in range(N_GROUPS):
        in_group = (lane < HEAD_DIM) if g == 0 else (lane >= HEAD_DIM)
        qg = jnp.concatenate(
            [jnp.where(in_group, q_all[:, j * LANES:(j + 1) * LANES], jnp.zeros((), BF16))
             for j in range(GROUP_SIZE)], axis=0)

        s_c = _dot_nt(qg, kc_ref[0])
        cmp_end = lax.broadcasted_iota(jnp.int32, (1, n_cmp_pad), 1) * CMP_STRIDE + (CMP_BLOCK - 1)
        p_c = _softmax_rows(s_c, cmp_end <= pos_r)
        branch_out[0].append(_dot(p_c.astype(BF16), vc_ref[0]))

        p_sum = p_c[0:Q_BLOCK]
        for j in range(1, GROUP_SIZE):
            p_sum = p_sum + p_c[j * Q_BLOCK:(j + 1) * Q_BLOCK]
        imp = _split_dot(p_sum, ov_ref[...])
        jj = lax.broadcasted_iota(jnp.int32, (1, n_sb), 1)
        cur = pos_q // SEL_BLOCK
        forced = (jj == 0) | (jj == cur) | (jj == cur - 1)
        valid = jj * SEL_BLOCK <= pos_q
        score = jnp.where(valid, jnp.where(forced, FORCE_SCORE, imp), -1.0)
        sel = jnp.zeros((Q_BLOCK, n_sb), F32)
        for _ in range(n_sel):
            m = jnp.max(score, axis=-1, keepdims=True)
            first = jnp.min(jnp.where(score == m, jj, n_sb), axis=-1, keepdims=True)
            pick = jj == first
            sel = jnp.where(pick, 1.0, sel)
            score = jnp.where(pick, -jnp.inf, score)
        sel_b = sel.astype(BF16)

        def sel_body(c, carry):
            m_run, l_run, acc = carry
            k0 = pl.multiple_of(c * ck, ck)
            kpos = k0 + lax.broadcasted_iota(jnp.int32, (1, ck), 1)
            blk_of_key = k0 // SEL_BLOCK + lax.broadcasted_iota(jnp.int32, (n_sb, ck), 1) // SEL_BLOCK
            expand = (blk_of_key == lax.broadcasted_iota(jnp.int32, (n_sb, ck), 0)).astype(BF16)
            chosen = _dot(sel_b, expand)
            chosen = jnp.concatenate([chosen] * GROUP_SIZE, axis=0)
            mask = (chosen > 0.5) & (kpos <= pos_r)
            s = _dot_nt(qg, ks_ref[0, pl.ds(k0, ck), :])
            s = jnp.where(mask, s, NEG)
            m_new = jnp.maximum(m_run, jnp.max(s, axis=-1, keepdims=True))
            alpha = jnp.exp(m_run - m_new)
            e = jnp.where(mask, jnp.exp(s - m_new), 0.0)
            l_new = alpha * l_run + jnp.sum(e, axis=-1, keepdims=True)
            acc = alpha * acc + _dot(e.astype(BF16), vs_ref[0, pl.ds(k0, ck), :])
            return m_new, l_new, acc

        n_chunks = (q0 + Q_BLOCK + ck - 1) // ck
        init = (jnp.full((rows, 1), NEG, F32), jnp.zeros((rows, 1), F32), jnp.zeros((rows, LANES), F32))
        _, l_s, acc_s = lax.fori_loop(0, n_chunks, sel_body, init)
        branch_out[1].append(acc_s / jnp.maximum(l_s, TINY))

        w0 = pl.multiple_of(jnp.clip(q0 + Q_BLOCK - win_len, 0, t_len - win_len), Q_BLOCK)
        kpos = w0 + lax.broadcasted_iota(jnp.int32, (1, win_len), 1)
        w_mask = (kpos <= pos_r) & (kpos > pos_r - WINDOW)
        p_w = _softmax_rows(_dot_nt(qg, kw_ref[0, pl.ds(w0, win_len), :]), w_mask)
        branch_out[2].append(_dot(p_w.astype(BF16), vw_ref[0, pl.ds(w0, win_len), :]))

    gates = gt_ref[0]
    low = lane < HEAD_DIM
    out = jnp.zeros((Q_BLOCK, NSA_WIDTH), F32)
    for br in range(3):
        g0, g1 = branch_out[br]
        merged = jnp.concatenate(
            [jnp.where(low, g0[j * Q_BLOCK:(j + 1) * Q_BLOCK], g1[j * Q_BLOCK:(j + 1) * Q_BLOCK])
             for j in range(GROUP_SIZE)], axis=-1)
        out = out + _split_dot(gates, ge_ref[br]) * merged
    o_ref[0] = out.astype(BF16)


def _nsa_attention(q, gates, kc, vc, ks, vs, kw, vw, *, ck):
    b, t, _ = q.shape
    n_cmp_pad = kc.shape[1]
    n_sb = t // SEL_BLOCK
    ck = min(ck, t)
    ci = np.arange(n_cmp_pad)[:, None] * CMP_STRIDE
    sj = np.arange(n_sb)[None, :] * SEL_BLOCK
    overlap = jnp.asarray((ci < sj + SEL_BLOCK) & (ci + CMP_BLOCK - 1 >= sj), BF16)
    perm = _q_perm()
    head_of_lane = perm // HEAD_DIM
    ge = np.zeros((3, GATE_PAD, NSA_WIDTH), np.float32)
    for c in range(3):
        ge[c, 3 * head_of_lane + c, np.arange(NSA_WIDTH)] = 1.0
    ge = jnp.asarray(ge, BF16)

    kv_spec = pl.BlockSpec((1, t, KV_WIDTH), lambda i, j: (i, 0, 0))
    cmp_spec = pl.BlockSpec((1, n_cmp_pad, KV_WIDTH), lambda i, j: (i, 0, 0))
    return pl.pallas_call(
        functools.partial(_nsa_kernel, t_len=t, ck=ck),
        grid=(b, t // Q_BLOCK),
        in_specs=[
            pl.BlockSpec((1, Q_BLOCK, NSA_WIDTH), lambda i, j: (i, j, 0)),
            pl.BlockSpec((1, Q_BLOCK, GATE_PAD), lambda i, j: (i, j, 0)),
            cmp_spec, cmp_spec, kv_spec, kv_spec, kv_spec, kv_spec,
            _const_spec((n_cmp_pad, n_sb)),
            _const_spec((3, GATE_PAD, NSA_WIDTH)),
        ],
        out_specs=pl.BlockSpec((1, Q_BLOCK, NSA_WIDTH), lambda i, j: (i, j, 0)),
        out_shape=jax.ShapeDtypeStruct((b, t, NSA_WIDTH), BF16),
        compiler_params=pltpu.CompilerParams(
            dimension_semantics=("arbitrary", "arbitrary"), vmem_limit_bytes=VMEM_LIMIT),
        name="nsa_attention",
    )(q, gates, kc, vc, ks, vs, kw, vw, overlap, ge)


def _out_proj_kernel(x_ref, a_ref, z_ref, wa_ref, wz_ref, o_ref):
    o_ref[0] = x_ref[0] + _dot(a_ref[0], wa_ref[...]) + _dot(z_ref[0], wz_ref[...])


def _out_proj(x, o_nsa, z, w_out, *, tm):
    b, t, d = x.shape
    wa = w_out[:NSA_WIDTH][_q_perm()].astype(BF16)
    wz = w_out[NSA_WIDTH:].astype(BF16)
    return pl.pallas_call(
        _out_proj_kernel,
        grid=(b, t // tm),
        in_specs=[
            pl.BlockSpec((1, tm, d), lambda i, j: (i, j, 0)),
            pl.BlockSpec((1, tm, NSA_WIDTH), lambda i, j: (i, j, 0)),
            pl.BlockSpec((1, tm, CONF_WIDTH), lambda i, j: (i, j, 0)),
            _const_spec((NSA_WIDTH, d)),
            _const_spec((CONF_WIDTH, d)),
        ],
        out_specs=pl.BlockSpec((1, tm, d), lambda i, j: (i, j, 0)),
        out_shape=jax.ShapeDtypeStruct(x.shape, F32),
        compiler_params=pltpu.CompilerParams(
            dimension_semantics=("arbitrary", "arbitrary"), vmem_limit_bytes=VMEM_LIMIT),
        name="out_proj",
    )(x, o_nsa, z, wa, wz)


def _tile(t, want):
    return want if t % want == 0 else t


def kernel(x, norm_mix, norm_ffn, ffn_gate, ffn_up, ffn_down, w_in_even, conv_a, pool_w, pool_scale,
           w_out_even, w_in_odd, q_norm, k_norm, cmp_pos, cmp_w1, cmp_w2, conf_dw, conf_dw_b,
           conf_ln_g, conf_ln_b, w_out_odd):
    depth = norm_mix.shape[0]
    t = x.shape[1]
    tm = _tile(t, 512)
    d_ff = ffn_gate.shape[2]
    chunk = 256 if d_ff % 256 == 0 else d_ff
    for i in range(depth):
        if i % 2 == 0:
            e = i // 2
            x = _even_mixer(x, norm_mix[i], w_in_even[e], conv_a[e], pool_w[e], pool_scale[e],
                            w_out_even[e], tm=tm)
        else:
            o = i // 2
            q, kc_tok, vc_tok, ks, vs, kw, vw, gates, z = _odd_proj(
                x, norm_mix[i], w_in_odd[o], q_norm[o], k_norm[o], conf_dw[o], conf_dw_b[o],
                conf_ln_g[o], conf_ln_b[o], tm=tm)
            kc, vc = _compress(kc_tok, vc_tok, cmp_pos[o], cmp_w1[o], cmp_w2[o], k_norm[o, 2])
            o_nsa = _nsa_attention(q, gates, kc, vc, ks, vs, kw, vw, ck=512)
            x = _out_proj(x, o_nsa, z, w_out_odd[o], tm=tm)
        x = _ffn(x, norm_ffn[i], ffn_gate[i], ffn_up[i], ffn_down[i], tm=tm, chunk=chunk)
    return x
```

```python
import functools

import numpy as np
import jax
import jax.numpy as jnp
from jax import lax
from jax.experimental import pallas as pl
from jax.experimental.pallas import tpu as pltpu

EPS = 1e-6
ROPE_THETA = 10000.0
HEAD_DIM = 64
HALF = HEAD_DIM // 2
LANES = 128
SUBLANES = 8
LOG2E = 1.4426950408889634
N_HEADS = 8
N_GROUPS = 2
GROUP_SIZE = N_HEADS // N_GROUPS
NSA_WIDTH = N_HEADS * HEAD_DIM
KV_WIDTH = N_GROUPS * HEAD_DIM
CONF_WIDTH = 512
CONF_K = 31
CONV_WIDTH = 512
POOL_WINDOWS = (2, 4, 8, 16)
POOL_GROUP = 128
SHORT_CONV_K = 3
CMP_BLOCK = 32
CMP_STRIDE = 16
CMP_HIDDEN = 4 * HEAD_DIM
SEL_BLOCK = 64
N_SELECT = 8
WINDOW = 512
Q_BLOCK = 128
FORCE_SCORE = 1e4
NEG = -1e30
M_FLOOR = -1e29
TINY = 1e-30
GATE_PAD = LANES
ODD_COLS = NSA_WIDTH + 6 * KV_WIDTH + GATE_PAD + 2 * CONF_WIDTH

VMEM_LIMIT = 56 * 1024 * 1024

BF16 = jnp.bfloat16
F32 = jnp.float32


def _dot(a, b):
    return jnp.dot(a, b, preferred_element_type=F32)


def _dot_nt(a, b):
    return lax.dot_general(a, b, (((1,), (1,)), ((), ())), preferred_element_type=F32)


def _split_dot(a, b):
    hi = a.astype(BF16)
    lo = (a - hi.astype(F32)).astype(BF16)
    return _dot(hi, b) + _dot(lo, b)


def _rms_rows(x, g):
    ms = jnp.mean(x * x, axis=-1, keepdims=True)
    return x * lax.rsqrt(ms + EPS) * g


def _sigmoid(x):
    return 1.0 / (1.0 + jnp.exp(-x))


def _const_spec(shape):
    nd = len(shape)
    return pl.BlockSpec(shape, lambda *_: (0,) * nd, pipeline_mode=pl.Buffered(1))


def _ffn_kernel(x_ref, a_ref, z_ref, wa_ref, wz_ref, g_ref, wg_ref, wu_ref, wd_ref, o_ref, x1_buf, *, n_chunks):
    x1_buf[...] = x_ref[0] + (_dot(a_ref[0], wa_ref[...]) + _dot(z_ref[0], wz_ref[...]))
    x = x1_buf[...]
    h = _rms_rows(x, g_ref[...]).astype(BF16)
    acc = x
    for c in range(n_chunks):
        gate = _dot(h, wg_ref[c])
        up = _dot(h, wu_ref[c])
        a = (gate * _sigmoid(gate) * up).astype(BF16)
        acc = acc + _dot(a, wd_ref[c])
    o_ref[0] = acc


def _ffn(x, a, z, w_out, g, wg, wu, wd, *, tm, chunk):
    b, t, d = x.shape
    wa_n, wz_n = a.shape[2], z.shape[2]
    dff = wg.shape[1]
    n_chunks = dff // chunk
    wg_c = wg.astype(BF16).reshape(d, n_chunks, chunk).transpose(1, 0, 2)
    wu_c = wu.astype(BF16).reshape(d, n_chunks, chunk).transpose(1, 0, 2)
    wd_c = wd.astype(BF16).reshape(n_chunks, chunk, d)
    return pl.pallas_call(
        functools.partial(_ffn_kernel, n_chunks=n_chunks),
        grid=(b, t // tm),
        in_specs=[
            pl.BlockSpec((1, tm, d), lambda i, j: (i, j, 0)),
            pl.BlockSpec((1, tm, wa_n), lambda i, j: (i, j, 0)),
            pl.BlockSpec((1, tm, wz_n), lambda i, j: (i, j, 0)),
            _const_spec((wa_n, d)),
            _const_spec((wz_n, d)),
            _const_spec((1, d)),
            _const_spec((n_chunks, d, chunk)),
            _const_spec((n_chunks, d, chunk)),
            _const_spec((n_chunks, chunk, d)),
        ],
        out_specs=pl.BlockSpec((1, tm, d), lambda i, j: (i, j, 0)),
        out_shape=jax.ShapeDtypeStruct(x.shape, F32),
        scratch_shapes=[pltpu.VMEM((tm, d), F32)],
        compiler_params=pltpu.CompilerParams(
            dimension_semantics=("arbitrary", "arbitrary"), vmem_limit_bytes=VMEM_LIMIT),
        name="ffn",
    )(x, a, z, w_out[:wa_n].astype(BF16), w_out[wa_n:].astype(BF16), g.reshape(1, d), wg_c, wu_c, wd_c)


def _even_kernel(x_ref, g_ref, win_ref, cw_ref, pw_ref, ps_ref, yc_ref, yp_ref,
                 cv_buf, vp_buf, *, tm):
    t = pl.program_id(1)
    x = x_ref[0]
    h = _rms_rows(x, g_ref[...]).astype(BF16)
    u = _dot(h, win_ref[...])
    b_gate = u[:, 0:CONV_WIDTH]
    cv = u[:, CONV_WIDTH:2 * CONV_WIDTH] * u[:, 2 * CONV_WIDTH:3 * CONV_WIDTH]
    vp = u[:, 3 * CONV_WIDTH:]

    @pl.when(t == 0)
    def _():
        cv_buf[0:8, :] = jnp.zeros((8, CONV_WIDTH), F32)
        vp_buf[0:16, :] = jnp.zeros((16, 4 * POOL_GROUP), F32)

    @pl.when(t > 0)
    def _():
        cv_buf[0:8, :] = cv_buf[tm:tm + 8, :]
        vp_buf[0:16, :] = vp_buf[tm:tm + 16, :]

    cv_buf[8:8 + tm, :] = cv
    vp_buf[16:16 + tm, :] = vp

    cw = cw_ref[...]
    conv = cw[2:3, :] * cv
    for k in range(SHORT_CONV_K - 1):
        off = 8 - (SHORT_CONV_K - 1) + k
        conv = conv + cw[k:k + 1, :] * cv_buf[off:off + tm, :]
    yc_ref[0] = (b_gate * conv).astype(BF16)

    pos = t * tm + lax.broadcasted_iota(jnp.int32, (tm, 1), 0)
    ys = []
    for gi, w in enumerate(POOL_WINDOWS):
        lo, hi = gi * POOL_GROUP, (gi + 1) * POOL_GROUP
        s = vp[:, lo:hi]
        for k in range(1, w):
            s = s + vp_buf[16 - k:16 - k + tm, lo:hi]
        cnt = jnp.minimum(pos + 1, w).astype(F32)
        pooled = s / cnt - vp[:, lo:hi]
        ys.append(_dot(pooled.astype(BF16), pw_ref[gi]))
    yp_ref[0] = (jnp.concatenate(ys, axis=-1) * ps_ref[...]).astype(BF16)


def _even_mixer(x, g, w_in, conv_w, pool_w, pool_scale, *, tm):
    b, t, d = x.shape
    n_in = w_in.shape[1]
    return pl.pallas_call(
        functools.partial(_even_kernel, tm=tm),
        grid=(b, t // tm),
        in_specs=[
            pl.BlockSpec((1, tm, d), lambda i, j: (i, j, 0)),
            _const_spec((1, d)),
            _const_spec((d, n_in)),
            _const_spec((SHORT_CONV_K, CONV_WIDTH)),
            _const_spec((len(POOL_WINDOWS), POOL_GROUP, POOL_GROUP)),
            _const_spec((1, 4 * POOL_GROUP)),
        ],
        out_specs=[pl.BlockSpec((1, tm, CONV_WIDTH), lambda i, j: (i, j, 0)),
                   pl.BlockSpec((1, tm, 4 * POOL_GROUP), lambda i, j: (i, j, 0))],
        out_shape=[jax.ShapeDtypeStruct((b, t, CONV_WIDTH), BF16),
                   jax.ShapeDtypeStruct((b, t, 4 * POOL_GROUP), BF16)],
        scratch_shapes=[pltpu.VMEM((tm + 8, CONV_WIDTH), F32),
                        pltpu.VMEM((tm + 16, 4 * POOL_GROUP), F32)],
        compiler_params=pltpu.CompilerParams(
            dimension_semantics=("arbitrary", "arbitrary"), vmem_limit_bytes=VMEM_LIMIT),
        name="even_mixer",
    )(x, g.reshape(1, d), w_in.astype(BF16), conv_w, pool_w.astype(BF16), pool_scale.reshape(1, -1))


def _rope_tile(x, cos, sin_signed, first_half):
    rot = jnp.where(first_half, pltpu.roll(x, LANES - HALF, 1), pltpu.roll(x, HALF, 1))
    return x * cos + rot * sin_signed


def _head_ms(x, ones_bd):
    return _split_dot(x * x, ones_bd) * (1.0 / HEAD_DIM)


def _odd_proj_kernel(x_ref, g_ref, win_ref, cos_ref, sin_ref, qn_ref, kn_ref, bd_ref,
                     dw_ref, dwb_ref, lng_ref, lnb_ref,
                     q_ref, kc_ref, vc_ref, ksa_ref, ksb_ref, vs_ref, kw_ref, vw_ref, gt_ref, z_ref,
                     z_buf, ph_buf, *, tm):
    t = pl.program_id(1)
    x = x_ref[0]
    h = _rms_rows(x, g_ref[...]).astype(BF16)
    u = _dot(h, win_ref[...])
    cos = cos_ref[...]
    sin = sin_ref[...]
    lane = lax.broadcasted_iota(jnp.int32, (1, LANES), 1)
    first_half = (lane % HEAD_DIM) < HALF
    bd = bd_ref[...]

    scale = LOG2E * HEAD_DIM ** -0.5
    q_parts = []
    for j in range(NSA_WIDTH // LANES):
        qj = u[:, j * LANES:(j + 1) * LANES]
        qj = qj * lax.rsqrt(_head_ms(qj, bd) + EPS) * qn_ref[...]
        q_parts.append(_rope_tile(qj, cos, sin, first_half) * scale)
    q_ref[0] = jnp.concatenate(q_parts, axis=-1).astype(BF16)

    o = NSA_WIDTH
    kc_ref[0] = _rope_tile(u[:, o:o + KV_WIDTH], cos, sin, first_half).astype(BF16)
    o += KV_WIDTH
    vc_ref[0] = u[:, o:o + KV_WIDTH].astype(BF16)
    o += KV_WIDTH
    k = u[:, o:o + KV_WIDTH]
    k = k * lax.rsqrt(_head_ms(k, bd) + EPS) * kn_ref[0:1, :]
    k = _rope_tile(k, cos, sin, first_half)
    blk = (t * tm + lax.broadcasted_iota(jnp.int32, (tm, 1), 0)) // SEL_BLOCK
    low = lane < HEAD_DIM
    ksa_ref[0] = jnp.where(low, k, jnp.where(lane - HEAD_DIM == blk, NEG, 0.0)).astype(BF16)
    ksb_ref[0] = jnp.where(low, jnp.where(lane == blk, NEG, 0.0), k).astype(BF16)
    o += KV_WIDTH
    vs_ref[0] = u[:, o:o + KV_WIDTH].astype(BF16)
    o += KV_WIDTH
    k = u[:, o:o + KV_WIDTH]
    k = k * lax.rsqrt(_head_ms(k, bd) + EPS) * kn_ref[1:2, :]
    kw_ref[0] = _rope_tile(k, cos, sin, first_half).astype(BF16)
    o += KV_WIDTH
    vw_ref[0] = u[:, o:o + KV_WIDTH].astype(BF16)
    o += KV_WIDTH
    gt_ref[0] = _sigmoid(u[:, o:o + GATE_PAD])
    o += GATE_PAD

    zz = u[:, o:o + CONF_WIDTH] * _sigmoid(u[:, o + CONF_WIDTH:o + 2 * CONF_WIDTH])
    halo = 32

    @pl.when(t == 0)
    def _():
        z_buf[0:halo, :] = jnp.zeros((halo, CONF_WIDTH), F32)

    @pl.when(t > 0)
    def _():
        z_buf[0:halo, :] = z_buf[tm:tm + halo, :]

    z_buf[halo:halo + tm, :] = zz
    for r in range(1, SUBLANES):
        ph_buf[r - 1] = z_buf[r:r + tm + halo - SUBLANES, :]
    dw = dw_ref[...]
    conv = dw[CONF_K - 1:CONF_K, :] * zz + dwb_ref[...]
    for kk in range(CONF_K - 1):
        off = halo - (CONF_K - 1) + kk
        a, r = off // SUBLANES * SUBLANES, off % SUBLANES
        tap = z_buf[a:a + tm, :] if r == 0 else ph_buf[r - 1, a:a + tm, :]
        conv = conv + dw[kk:kk + 1, :] * tap
    mu = jnp.mean(conv, axis=-1, keepdims=True)
    cen = conv - mu
    var = jnp.mean(cen * cen, axis=-1, keepdims=True)
    y = cen * lax.rsqrt(var + EPS) * lng_ref[...] + lnb_ref[...]
    z_ref[0] = (y * _sigmoid(y)).astype(BF16)


def _q_perm():
    p = np.arange(NSA_WIDTH)
    j, within = p // LANES, p % LANES
    head = np.where(within < HEAD_DIM, j, GROUP_SIZE + j)
    return head * HEAD_DIM + within % HEAD_DIM


def _odd_proj(x, g, w_in, q_norm, k_norm, conf_dw, conf_dw_b, conf_ln_g, conf_ln_b, *, tm):
    b, t, d = x.shape
    perm = _q_perm()
    n_kv = 6 * KV_WIDTH
    n_gate = 3 * N_HEADS
    w_q = w_in[:, :NSA_WIDTH][:, perm]
    w_kv = w_in[:, NSA_WIDTH:NSA_WIDTH + n_kv]
    w_gt = jnp.pad(w_in[:, NSA_WIDTH + n_kv:NSA_WIDTH + n_kv + n_gate], ((0, 0), (0, GATE_PAD - n_gate)))
    w_cf = w_in[:, NSA_WIDTH + n_kv + n_gate:]
    w_all = jnp.concatenate([w_q, w_kv, w_gt, w_cf], axis=1).astype(BF16)

    inv = 1.0 / (ROPE_THETA ** (jnp.arange(0, HEAD_DIM, 2, dtype=F32) / HEAD_DIM))
    ang = jnp.arange(t, dtype=F32)[:, None] * inv[None, :]
    cos = jnp.tile(jnp.cos(ang), (1, LANES // HALF))
    sin = jnp.tile(jnp.sin(ang), (1, LANES // HALF))
    lane = np.arange(LANES)
    sin_signed = jnp.where((lane % HEAD_DIM < HALF)[None, :], -sin, sin)

    ones_bd = jnp.asarray((lane[:, None] // HEAD_DIM == lane[None, :] // HEAD_DIM), BF16)
    qn = jnp.tile(q_norm, LANES // HEAD_DIM).reshape(1, LANES)
    kn = jnp.tile(k_norm[:2], (1, LANES // HEAD_DIM))

    tok = lambda w, dt: jax.ShapeDtypeStruct((b, t, w), dt)
    tok_spec = lambda w: pl.BlockSpec((1, tm, w), lambda i, j: (i, j, 0))
    return pl.pallas_call(
        functools.partial(_odd_proj_kernel, tm=tm),
        grid=(b, t // tm),
        in_specs=[
            tok_spec(d),
            _const_spec((1, d)),
            _const_spec((d, ODD_COLS)),
            pl.BlockSpec((tm, LANES), lambda i, j: (j, 0)),
            pl.BlockSpec((tm, LANES), lambda i, j: (j, 0)),
            _const_spec((1, LANES)),
            _const_spec((2, LANES)),
            _const_spec((LANES, LANES)),
            _const_spec((CONF_K, CONF_WIDTH)),
            _const_spec((1, CONF_WIDTH)),
            _const_spec((1, CONF_WIDTH)),
            _const_spec((1, CONF_WIDTH)),
        ],
        out_specs=[tok_spec(NSA_WIDTH)] + [tok_spec(KV_WIDTH)] * 7 + [tok_spec(GATE_PAD), tok_spec(CONF_WIDTH)],
        out_shape=[tok(NSA_WIDTH, BF16)] + [tok(KV_WIDTH, BF16)] * 7 + [tok(GATE_PAD, F32), tok(CONF_WIDTH, BF16)],
        scratch_shapes=[pltpu.VMEM((tm + 32, CONF_WIDTH), F32),
                        pltpu.VMEM((SUBLANES - 1, tm + 32 - SUBLANES, CONF_WIDTH), F32)],
        compiler_params=pltpu.CompilerParams(
            dimension_semantics=("arbitrary", "arbitrary"), vmem_limit_bytes=VMEM_LIMIT),
        name="odd_proj",
    )(x, g.reshape(1, d), w_all, cos, sin_signed, qn, kn, ones_bd,
      conf_dw, conf_dw_b.reshape(1, -1), conf_ln_g.reshape(1, -1), conf_ln_b.reshape(1, -1))


def _gelu_tanh(x):
    return 0.5 * x * (1.0 + jnp.tanh(np.sqrt(2.0 / np.pi) * (x + 0.044715 * (x * x * x))))


def _compress_kernel(kt_ref, vt_ref, w1_ref, pos_ref, w2_ref, kn_ref, kc_ref, vc_ref, sh_buf, *, n_rows):
    hw = N_GROUPS * CMP_HIDDEN
    outs = []
    for s, tok_ref in enumerate((kt_ref, vt_ref)):
        rows = tok_ref[0]
        first = _dot(rows, w1_ref[s, 0])
        second = _dot(rows, w1_ref[s, 1])
        sh_buf[0:n_rows, :] = second
        sh_buf[n_rows:n_rows + 8, :] = jnp.zeros((8, hw), F32)
        bias = _dot(pos_ref[s, 0], w1_ref[s, 0]) + _dot(pos_ref[s, 1], w1_ref[s, 1])
        hid = first + sh_buf[1:n_rows + 1, :] + bias[0:1, :]
        act = _gelu_tanh(hid).astype(BF16)
        per_group = [_dot(act[:, gi * CMP_HIDDEN:(gi + 1) * CMP_HIDDEN], w2_ref[s]) for gi in range(N_GROUPS)]
        outs.append(per_group)
    kc = [_rms_rows(o, kn_ref[...]) for o in outs[0]]
    kc_ref[0] = jnp.concatenate(kc, axis=-1).astype(BF16)
    vc_ref[0] = jnp.concatenate(outs[1], axis=-1).astype(BF16)


def _compress(kc_tok, vc_tok, cmp_pos, cmp_w1, cmp_w2, k_norm2):
    b, t, _ = kc_tok.shape
    n_rows = t // CMP_STRIDE
    row_w = CMP_STRIDE * KV_WIDTH
    hw = N_GROUPS * CMP_HIDDEN
    w1 = cmp_w1.reshape(2, 2, CMP_STRIDE, HEAD_DIM, CMP_HIDDEN)
    eye = jnp.eye(N_GROUPS, dtype=cmp_w1.dtype)
    w1x = jnp.einsum('shrdc,gk->shrgdkc', w1, eye).reshape(2, 2, row_w, hw).astype(BF16)
    pos = cmp_pos.reshape(2, 2, CMP_STRIDE, 1, HEAD_DIM)
    posx = jnp.broadcast_to(pos, (2, 2, CMP_STRIDE, N_GROUPS, HEAD_DIM)).reshape(2, 2, 1, row_w)
    posx = jnp.broadcast_to(posx, (2, 2, 8, row_w)).astype(BF16)
    out = jax.ShapeDtypeStruct((b, n_rows, KV_WIDTH), BF16)
    return pl.pallas_call(
        functools.partial(_compress_kernel, n_rows=n_rows),
        grid=(b,),
        in_specs=[
            pl.BlockSpec((1, n_rows, row_w), lambda i: (i, 0, 0)),
            pl.BlockSpec((1, n_rows, row_w), lambda i: (i, 0, 0)),
            _const_spec((2, 2, row_w, hw)),
            _const_spec((2, 2, 8, row_w)),
            _const_spec((2, CMP_HIDDEN, HEAD_DIM)),
            _const_spec((1, HEAD_DIM)),
        ],
        out_specs=[pl.BlockSpec((1, n_rows, KV_WIDTH), lambda i: (i, 0, 0))] * 2,
        out_shape=[out, out],
        scratch_shapes=[pltpu.VMEM((n_rows + 8, hw), F32)],
        compiler_params=pltpu.CompilerParams(
            dimension_semantics=("arbitrary",), vmem_limit_bytes=VMEM_LIMIT),
        name="compress",
    )(kc_tok.reshape(b, n_rows, row_w), vc_tok.reshape(b, n_rows, row_w), w1x, posx,
      cmp_w2.astype(BF16), k_norm2.reshape(1, HEAD_DIM))


def _add_per_query(s, bias):
    return jnp.concatenate(
        [s[j * Q_BLOCK:(j + 1) * Q_BLOCK] + bias for j in range(GROUP_SIZE)], axis=0)


def _exp_rows(s):
    m = jnp.maximum(jnp.max(s, axis=-1, keepdims=True), M_FLOOR)
    e = jnp.exp2(s - m)
    return e, jnp.sum(e, axis=-1, keepdims=True)


def _nsa_kernel(q_ref, gt_ref, kc_ref, vc_ref, ksa_ref, ksb_ref, vs_ref, kw_ref, vw_ref, ov_ref, ge_ref,
                o_ref, *, t_len, ck):
    qb = pl.program_id(1)
    q0 = qb * Q_BLOCK
    rows = GROUP_SIZE * Q_BLOCK
    n_cmp_pad = kc_ref.shape[1]
    n_sb = t_len // SEL_BLOCK
    n_sel = min(N_SELECT, n_sb)
    win_len = min(WINDOW + Q_BLOCK, t_len)

    q_all = q_ref[0]
    lane = lax.broadcasted_iota(jnp.int32, (1, LANES), 1)
    low = lane < HEAD_DIM
    pos_q = q0 + lax.broadcasted_iota(jnp.int32, (Q_BLOCK, 1), 0)

    cmp_end = lax.broadcasted_iota(jnp.int32, (1, n_cmp_pad), 1) * CMP_STRIDE + (CMP_BLOCK - 1)
    bias_c = jnp.where(cmp_end <= pos_q, 0.0, NEG)
    w0 = pl.multiple_of(jnp.maximum(q0 + Q_BLOCK - win_len, 0), Q_BLOCK)
    kpos_w = w0 + lax.broadcasted_iota(jnp.int32, (1, win_len), 1)
    bias_w = jnp.where((kpos_w <= pos_q) & (kpos_w > pos_q - WINDOW), 0.0, NEG)
    n_full = q0 // ck
    k_tail = pl.multiple_of(n_full * ck, ck)
    kpos_t = k_tail + lax.broadcasted_iota(jnp.int32, (1, ck), 1)
    bias_t = jnp.where(kpos_t <= pos_q, 0.0, NEG)

    ks_refs = (ksa_ref, ksb_ref)
    q_sel = []
    out_c, out_w = [], []
    for g in range(N_GROUPS):
        is_q = low if g == 0 else jnp.logical_not(low)
        q_tiles = [q_all[:, j * LANES:(j + 1) * LANES] for j in range(GROUP_SIZE)]
        qz = jnp.concatenate([jnp.where(is_q, qt, jnp.zeros((), BF16)) for qt in q_tiles], axis=0)

        e_c, l_c = _exp_rows(_add_per_query(_dot_nt(qz, kc_ref[0]), bias_c))
        p_c = e_c * (1.0 / jnp.maximum(l_c, TINY))
        out_c.append(_dot(p_c.astype(BF16), vc_ref[0]))

        p_sum = p_c[0:Q_BLOCK]
        for j in range(1, GROUP_SIZE):
            p_sum = p_sum + p_c[j * Q_BLOCK:(j + 1) * Q_BLOCK]
        imp = _split_dot(p_sum, ov_ref[g])
        jj = lane - (HEAD_DIM if g == 0 else 0)
        cur = pos_q // SEL_BLOCK
        forced = (jj == 0) | (jj == cur) | (jj == cur - 1)
        valid = jj * SEL_BLOCK <= pos_q
        score = jnp.where(valid, jnp.where(forced, FORCE_SCORE, imp), -1.0)
        score = jnp.where((jj >= 0) & (jj < n_sb), score, -jnp.inf)
        not_sel = jnp.ones((Q_BLOCK, LANES), F32)
        for _ in range(n_sel):
            m = jnp.max(score, axis=-1, keepdims=True)
            first = jnp.min(jnp.where(score == m, jj, LANES), axis=-1, keepdims=True)
            pick = jj == first
            not_sel = jnp.where(pick, 0.0, not_sel)
            score = jnp.where(pick, -jnp.inf, score)
        not_sel = not_sel.astype(BF16)
        q_sel.append(jnp.concatenate([jnp.where(is_q, qt, not_sel) for qt in q_tiles], axis=0))

        e_w, l_w = _exp_rows(_add_per_query(_dot_nt(qz, kw_ref[0, pl.ds(w0, win_len), :]), bias_w))
        out_w.append(_dot(e_w.astype(BF16), vw_ref[0, pl.ds(w0, win_len), :]) / jnp.maximum(l_w, TINY))

    def sel_chunk(state, k0, bias):
        new = []
        for g in range(N_GROUPS):
            m_run, l_run, acc = state[g]
            s = _dot_nt(q_sel[g], ks_refs[g][0, pl.ds(k0, ck), :])
            if bias is not None:
                s = _add_per_query(s, bias)
            m_new = jnp.maximum(m_run, jnp.max(s, axis=-1, keepdims=True))
            alpha = jnp.exp2(m_run - m_new)
            e = jnp.exp2(s - m_new)
            l_new = alpha * l_run + jnp.sum(e, axis=-1, keepdims=True)
            acc = alpha * acc + _dot(e.astype(BF16), vs_ref[0, pl.ds(k0, ck), :])
            new.append((m_new, l_new, acc))
        return tuple(new)

    one = (jnp.full((rows, 1), M_FLOOR, F32), jnp.zeros((rows, 1), F32), jnp.zeros((rows, LANES), F32))
    state = lax.fori_loop(
        0, n_full, lambda c, st: sel_chunk(st, pl.multiple_of(c * ck, ck), None), (one,) * N_GROUPS)
    state = sel_chunk(state, k_tail, bias_t)
    out_s = [acc / jnp.maximum(l_run, TINY) for _, l_run, acc in state]

    branch_out = [out_c, out_s, out_w]
    gates = gt_ref[0]
    out = jnp.zeros((Q_BLOCK, NSA_WIDTH), F32)
    for br in range(3):
        g0, g1 = branch_out[br]
        merged = jnp.concatenate(
            [jnp.where(low, g0[j * Q_BLOCK:(j + 1) * Q_BLOCK], g1[j * Q_BLOCK:(j + 1) * Q_BLOCK])
             for j in range(GROUP_SIZE)], axis=-1)
        out = out + _split_dot(gates, ge_ref[br]) * merged
    o_ref[0] = out.astype(BF16)


def _nsa_attention(q, gates, kc, vc, ksa, ksb, vs, kw, vw, *, ck):
    b, t, _ = q.shape
    n_cmp_pad = kc.shape[1]
    n_sb = t // SEL_BLOCK
    assert n_sb <= HEAD_DIM, "selection blocks must fit the 64 spare lanes of a kv-group tile"
    ck = min(ck, t)
    ci = np.arange(n_cmp_pad)[:, None] * CMP_STRIDE
    sj = np.arange(n_sb)[None, :] * SEL_BLOCK
    ov = (ci < sj + SEL_BLOCK) & (ci + CMP_BLOCK - 1 >= sj)
    overlap = np.zeros((N_GROUPS, n_cmp_pad, LANES), np.float32)
    overlap[0, :, HEAD_DIM:HEAD_DIM + n_sb] = ov
    overlap[1, :, 0:n_sb] = ov
    overlap = jnp.asarray(overlap, BF16)
    perm = _q_perm()
    head_of_lane = perm // HEAD_DIM
    ge = np.zeros((3, GATE_PAD, NSA_WIDTH), np.float32)
    for c in range(3):
        ge[c, 3 * head_of_lane + c, np.arange(NSA_WIDTH)] = 1.0
    ge = jnp.asarray(ge, BF16)

    kv_spec = pl.BlockSpec((1, t, KV_WIDTH), lambda i, j: (i, 0, 0))
    cmp_spec = pl.BlockSpec((1, n_cmp_pad, KV_WIDTH), lambda i, j: (i, 0, 0))
    return pl.pallas_call(
        functools.partial(_nsa_kernel, t_len=t, ck=ck),
        grid=(b, t // Q_BLOCK),
        in_specs=[
            pl.BlockSpec((1, Q_BLOCK, NSA_WIDTH), lambda i, j: (i, j, 0)),
            pl.BlockSpec((1, Q_BLOCK, GATE_PAD), lambda i, j: (i, j, 0)),
            cmp_spec, cmp_spec, kv_spec, kv_spec, kv_spec, kv_spec, kv_spec,
            _const_spec((N_GROUPS, n_cmp_pad, LANES)),
            _const_spec((3, GATE_PAD, NSA_WIDTH)),
        ],
        out_specs=pl.BlockSpec((1, Q_BLOCK, NSA_WIDTH), lambda i, j: (i, j, 0)),
        out_shape=jax.ShapeDtypeStruct((b, t, NSA_WIDTH), BF16),
        compiler_params=pltpu.CompilerParams(
            dimension_semantics=("arbitrary", "arbitrary"), vmem_limit_bytes=VMEM_LIMIT),
        name="nsa_attention",
    )(q, gates, kc, vc, ksa, ksb, vs, kw, vw, overlap, ge)


def _tile(t, want):
    return want if t % want == 0 else t


def kernel(x, norm_mix, norm_ffn, ffn_gate, ffn_up, ffn_down, w_in_even, conv_a, pool_w, pool_scale,
           w_out_even, w_in_odd, q_norm, k_norm, cmp_pos, cmp_w1, cmp_w2, conf_dw, conf_dw_b,
           conf_ln_g, conf_ln_b, w_out_odd):
    depth = norm_mix.shape[0]
    t = x.shape[1]
    tm = _tile(t, 512)
    d_ff = ffn_gate.shape[2]
    chunk = 256 if d_ff % 256 == 0 else d_ff
    for i in range(depth):
        if i % 2 == 0:
            e = i // 2
            a, z = _even_mixer(x, norm_mix[i], w_in_even[e], conv_a[e], pool_w[e], pool_scale[e], tm=tm)
            w_out = w_out_even[e]
        else:
            o = i // 2
            q, kc_tok, vc_tok, ksa, ksb, vs, kw, vw, gates, z = _odd_proj(
                x, norm_mix[i], w_in_odd[o], q_norm[o], k_norm[o], conf_dw[o], conf_dw_b[o],
                conf_ln_g[o], conf_ln_b[o], tm=tm)
            kc, vc = _compress(kc_tok, vc_tok, cmp_pos[o], cmp_w1[o], cmp_w2[o], k_norm[o, 2])
            a = _nsa_attention(q, gates, kc, vc, ksa, ksb, vs, kw, vw, ck=512)
            w_out = jnp.concatenate([w_out_odd[o][:NSA_WIDTH][_q_perm()], w_out_odd[o][NSA_WIDTH:]], axis=0)
        x = _ffn(x, a, z, w_out, norm_ffn[i], ffn_gate[i], ffn_up[i], ffn_down[i], tm=tm, chunk=chunk)
    return x
```

```python
import functools

import numpy as np
import jax
import jax.numpy as jnp
from jax import lax
from jax.experimental import pallas as pl
from jax.experimental.pallas import tpu as pltpu

EPS = 1e-6
ROPE_THETA = 10000.0
HEAD_DIM = 64
HALF = HEAD_DIM // 2
LANES = 128
SUBLANES = 8
LOG2E = 1.4426950408889634
N_HEADS = 8
N_GROUPS = 2
GROUP_SIZE = N_HEADS // N_GROUPS
NSA_WIDTH = N_HEADS * HEAD_DIM
KV_WIDTH = N_GROUPS * HEAD_DIM
CONF_WIDTH = 512
CONF_K = 31
CONV_WIDTH = 512
POOL_WINDOWS = (2, 4, 8, 16)
POOL_GROUP = 128
SHORT_CONV_K = 3
CMP_BLOCK = 32
CMP_STRIDE = 16
CMP_HIDDEN = 4 * HEAD_DIM
SEL_BLOCK = 64
N_SELECT = 8
N_FORCED = 3
WINDOW = 512
Q_BLOCK = 128
FORCE_SCORE = 1e4
NEG = -1e30
M_FLOOR = -1e29
TINY = 1e-30
GATE_PAD = LANES
ODD_COLS = NSA_WIDTH + 6 * KV_WIDTH + GATE_PAD + 2 * CONF_WIDTH

VMEM_LIMIT = 56 * 1024 * 1024

BF16 = jnp.bfloat16
F32 = jnp.float32


def _dot(a, b):
    return jnp.dot(a, b, preferred_element_type=F32)


def _dot_nt(a, b):
    return lax.dot_general(a, b, (((1,), (1,)), ((), ())), preferred_element_type=F32)


def _split_dot(a, b):
    hi = a.astype(BF16)
    lo = (a - hi.astype(F32)).astype(BF16)
    return _dot(hi, b) + _dot(lo, b)


def _rms_rows(x, g):
    ms = jnp.mean(x * x, axis=-1, keepdims=True)
    return x * lax.rsqrt(ms + EPS) * g


def _sigmoid(x):
    return 1.0 / (1.0 + jnp.exp(-x))


def _const_spec(shape):
    nd = len(shape)
    return pl.BlockSpec(shape, lambda *_: (0,) * nd, pipeline_mode=pl.Buffered(1))


def _ffn_kernel(x_ref, a_ref, z_ref, wa_ref, wz_ref, g_ref, wg_ref, wu_ref, wd_ref, o_ref, x1_buf, *, n_chunks):
    x1_buf[...] = x_ref[0] + (_dot(a_ref[0], wa_ref[...]) + _dot(z_ref[0], wz_ref[...]))
    x = x1_buf[...]
    h = _rms_rows(x, g_ref[...]).astype(BF16)
    acc = x
    for c in range(n_chunks):
        gate = _dot(h, wg_ref[c])
        up = _dot(h, wu_ref[c])
        a = (gate * _sigmoid(gate) * up).astype(BF16)
        acc = acc + _dot(a, wd_ref[c])
    o_ref[0] = acc


def _ffn(x, a, z, w_out, g, wg, wu, wd, *, tm, chunk):
    b, t, d = x.shape
    wa_n, wz_n = a.shape[2], z.shape[2]
    dff = wg.shape[1]
    n_chunks = dff // chunk
    wg_c = wg.astype(BF16).reshape(d, n_chunks, chunk).transpose(1, 0, 2)
    wu_c = wu.astype(BF16).reshape(d, n_chunks, chunk).transpose(1, 0, 2)
    wd_c = wd.astype(BF16).reshape(n_chunks, chunk, d)
    return pl.pallas_call(
        functools.partial(_ffn_kernel, n_chunks=n_chunks),
        grid=(b, t // tm),
        in_specs=[
            pl.BlockSpec((1, tm, d), lambda i, j: (i, j, 0)),
            pl.BlockSpec((1, tm, wa_n), lambda i, j: (i, j, 0)),
            pl.BlockSpec((1, tm, wz_n), lambda i, j: (i, j, 0)),
            _const_spec((wa_n, d)),
            _const_spec((wz_n, d)),
            _const_spec((1, d)),
            _const_spec((n_chunks, d, chunk)),
            _const_spec((n_chunks, d, chunk)),
            _const_spec((n_chunks, chunk, d)),
        ],
        out_specs=pl.BlockSpec((1, tm, d), lambda i, j: (i, j, 0)),
        out_shape=jax.ShapeDtypeStruct(x.shape, F32),
        scratch_shapes=[pltpu.VMEM((tm, d), F32)],
        compiler_params=pltpu.CompilerParams(
            dimension_semantics=("arbitrary", "arbitrary"), vmem_limit_bytes=VMEM_LIMIT),
        name="ffn",
    )(x, a, z, w_out[:wa_n].astype(BF16), w_out[wa_n:].astype(BF16), g.reshape(1, d), wg_c, wu_c, wd_c)


def _even_kernel(x_ref, g_ref, win_ref, cw_ref, pw_ref, ps_ref, yc_ref, yp_ref,
                 cv_buf, vp_buf, *, tm):
    t = pl.program_id(1)
    x = x_ref[0]
    h = _rms_rows(x, g_ref[...]).astype(BF16)
    u = _dot(h, win_ref[...])
    b_gate = u[:, 0:CONV_WIDTH]
    cv = u[:, CONV_WIDTH:2 * CONV_WIDTH] * u[:, 2 * CONV_WIDTH:3 * CONV_WIDTH]
    vp = u[:, 3 * CONV_WIDTH:]

    @pl.when(t == 0)
    def _():
        cv_buf[0:8, :] = jnp.zeros((8, CONV_WIDTH), F32)
        vp_buf[0:16, :] = jnp.zeros((16, 4 * POOL_GROUP), F32)

    @pl.when(t > 0)
    def _():
        cv_buf[0:8, :] = cv_buf[tm:tm + 8, :]
        vp_buf[0:16, :] = vp_buf[tm:tm + 16, :]

    cv_buf[8:8 + tm, :] = cv
    vp_buf[16:16 + tm, :] = vp

    cw = cw_ref[...]
    conv = cw[2:3, :] * cv
    for k in range(SHORT_CONV_K - 1):
        off = 8 - (SHORT_CONV_K - 1) + k
        conv = conv + cw[k:k + 1, :] * cv_buf[off:off + tm, :]
    yc_ref[0] = (b_gate * conv).astype(BF16)

    pos = t * tm + lax.broadcasted_iota(jnp.int32, (tm, 1), 0)
    ys = []
    for gi, w in enumerate(POOL_WINDOWS):
        lo, hi = gi * POOL_GROUP, (gi + 1) * POOL_GROUP
        s = vp[:, lo:hi]
        for k in range(1, w):
            s = s + vp_buf[16 - k:16 - k + tm, lo:hi]
        cnt = jnp.minimum(pos + 1, w).astype(F32)
        pooled = s / cnt - vp[:, lo:hi]
        ys.append(_dot(pooled.astype(BF16), pw_ref[gi]))
    yp_ref[0] = (jnp.concatenate(ys, axis=-1) * ps_ref[...]).astype(BF16)


def _even_mixer(x, g, w_in, conv_w, pool_w, pool_scale, *, tm):
    b, t, d = x.shape
    n_in = w_in.shape[1]
    return pl.pallas_call(
        functools.partial(_even_kernel, tm=tm),
        grid=(b, t // tm),
        in_specs=[
            pl.BlockSpec((1, tm, d), lambda i, j: (i, j, 0)),
            _const_spec((1, d)),
            _const_spec((d, n_in)),
            _const_spec((SHORT_CONV_K, CONV_WIDTH)),
            _const_spec((len(POOL_WINDOWS), POOL_GROUP, POOL_GROUP)),
            _const_spec((1, 4 * POOL_GROUP)),
        ],
        out_specs=[pl.BlockSpec((1, tm, CONV_WIDTH), lambda i, j: (i, j, 0)),
                   pl.BlockSpec((1, tm, 4 * POOL_GROUP), lambda i, j: (i, j, 0))],
        out_shape=[jax.ShapeDtypeStruct((b, t, CONV_WIDTH), BF16),
                   jax.ShapeDtypeStruct((b, t, 4 * POOL_GROUP), BF16)],
        scratch_shapes=[pltpu.VMEM((tm + 8, CONV_WIDTH), F32),
                        pltpu.VMEM((tm + 16, 4 * POOL_GROUP), F32)],
        compiler_params=pltpu.CompilerParams(
            dimension_semantics=("arbitrary", "arbitrary"), vmem_limit_bytes=VMEM_LIMIT),
        name="even_mixer",
    )(x, g.reshape(1, d), w_in.astype(BF16), conv_w, pool_w.astype(BF16), pool_scale.reshape(1, -1))


def _rope_tile(x, cos, sin_signed, first_half):
    rot = jnp.where(first_half, pltpu.roll(x, LANES - HALF, 1), pltpu.roll(x, HALF, 1))
    return x * cos + rot * sin_signed


def _head_ms(x, ones_bd):
    return _split_dot(x * x, ones_bd) * (1.0 / HEAD_DIM)


def _odd_proj_kernel(x_ref, g_ref, win_ref, cos_ref, sin_ref, qn_ref, kn_ref, bd_ref,
                     dw_ref, dwb_ref, lng_ref, lnb_ref,
                     q_ref, kc_ref, vc_ref, ksa_ref, ksb_ref, vs_ref, kw_ref, vw_ref, gt_ref, z_ref,
                     z_buf, ph_buf, *, tm):
    t = pl.program_id(1)
    x = x_ref[0]
    h = _rms_rows(x, g_ref[...]).astype(BF16)
    u = _dot(h, win_ref[...])
    cos = cos_ref[...]
    sin = sin_ref[...]
    lane = lax.broadcasted_iota(jnp.int32, (1, LANES), 1)
    first_half = (lane % HEAD_DIM) < HALF
    bd = bd_ref[...]

    scale = LOG2E * HEAD_DIM ** -0.5
    q_parts = []
    for j in range(NSA_WIDTH // LANES):
        qj = u[:, j * LANES:(j + 1) * LANES]
        qj = qj * lax.rsqrt(_head_ms(qj, bd) + EPS) * qn_ref[...]
        q_parts.append(_rope_tile(qj, cos, sin, first_half) * scale)
    q_ref[0] = jnp.concatenate(q_parts, axis=-1).astype(BF16)

    o = NSA_WIDTH
    kc_ref[0] = _rope_tile(u[:, o:o + KV_WIDTH], cos, sin, first_half).astype(BF16)
    o += KV_WIDTH
    vc_ref[0] = u[:, o:o + KV_WIDTH].astype(BF16)
    o += KV_WIDTH
    k = u[:, o:o + KV_WIDTH]
    k = k * lax.rsqrt(_head_ms(k, bd) + EPS) * kn_ref[0:1, :]
    k = _rope_tile(k, cos, sin, first_half)
    blk = (t * tm + lax.broadcasted_iota(jnp.int32, (tm, 1), 0)) // SEL_BLOCK
    low = lane < HEAD_DIM
    ksa_ref[0] = jnp.where(low, k, jnp.where(lane - HEAD_DIM == blk, NEG, 0.0)).astype(BF16)
    ksb_ref[0] = jnp.where(low, jnp.where(lane == blk, NEG, 0.0), k).astype(BF16)
    o += KV_WIDTH
    vs_ref[0] = u[:, o:o + KV_WIDTH].T.astype(BF16)
    o += KV_WIDTH
    k = u[:, o:o + KV_WIDTH]
    k = k * lax.rsqrt(_head_ms(k, bd) + EPS) * kn_ref[1:2, :]
    kw_ref[0] = _rope_tile(k, cos, sin, first_half).astype(BF16)
    o += KV_WIDTH
    vw_ref[0] = u[:, o:o + KV_WIDTH].T.astype(BF16)
    o += KV_WIDTH
    gt_ref[0] = _sigmoid(u[:, o:o + GATE_PAD])
    o += GATE_PAD

    zz = u[:, o:o + CONF_WIDTH] * _sigmoid(u[:, o + CONF_WIDTH:o + 2 * CONF_WIDTH])
    halo = 32

    @pl.when(t == 0)
    def _():
        z_buf[0:halo, :] = jnp.zeros((halo, CONF_WIDTH), F32)

    @pl.when(t > 0)
    def _():
        z_buf[0:halo, :] = z_buf[tm:tm + halo, :]

    z_buf[halo:halo + tm, :] = zz
    for r in range(1, SUBLANES):
        ph_buf[r - 1] = z_buf[r:r + tm + halo - SUBLANES, :]
    dw = dw_ref[...]
    conv = dw[CONF_K - 1:CONF_K, :] * zz + dwb_ref[...]
    for kk in range(CONF_K - 1):
        off = halo - (CONF_K - 1) + kk
        a, r = off // SUBLANES * SUBLANES, off % SUBLANES
        tap = z_buf[a:a + tm, :] if r == 0 else ph_buf[r - 1, a:a + tm, :]
        conv = conv + dw[kk:kk + 1, :] * tap
    mu = jnp.mean(conv, axis=-1, keepdims=True)
    cen = conv - mu
    var = jnp.mean(cen * cen, axis=-1, keepdims=True)
    y = cen * lax.rsqrt(var + EPS) * lng_ref[...] + lnb_ref[...]
    z_ref[0] = (y * _sigmoid(y)).astype(BF16)


def _q_perm():
    p = np.arange(NSA_WIDTH)
    j, within = p // LANES, p % LANES
    head = np.where(within < HEAD_DIM, j, GROUP_SIZE + j)
    return head * HEAD_DIM + within % HEAD_DIM


def _odd_proj(x, g, w_in, q_norm, k_norm, conf_dw, conf_dw_b, conf_ln_g, conf_ln_b, *, tm):
    b, t, d = x.shape
    perm = _q_perm()
    n_kv = 6 * KV_WIDTH
    n_gate = 3 * N_HEADS
    w_q = w_in[:, :NSA_WIDTH][:, perm]
    w_kv = w_in[:, NSA_WIDTH:NSA_WIDTH + n_kv]
    w_gt = jnp.pad(w_in[:, NSA_WIDTH + n_kv:NSA_WIDTH + n_kv + n_gate], ((0, 0), (0, GATE_PAD - n_gate)))
    w_cf = w_in[:, NSA_WIDTH + n_kv + n_gate:]
    w_all = jnp.concatenate([w_q, w_kv, w_gt, w_cf], axis=1).astype(BF16)

    inv = 1.0 / (ROPE_THETA ** (jnp.arange(0, HEAD_DIM, 2, dtype=F32) / HEAD_DIM))
    ang = jnp.arange(t, dtype=F32)[:, None] * inv[None, :]
    cos = jnp.tile(jnp.cos(ang), (1, LANES // HALF))
    sin = jnp.tile(jnp.sin(ang), (1, LANES // HALF))
    lane = np.arange(LANES)
    sin_signed = jnp.where((lane % HEAD_DIM < HALF)[None, :], -sin, sin)

    ones_bd = jnp.asarray((lane[:, None] // HEAD_DIM == lane[None, :] // HEAD_DIM), BF16)
    qn = jnp.tile(q_norm, LANES // HEAD_DIM).reshape(1, LANES)
    kn = jnp.tile(k_norm[:2], (1, LANES // HEAD_DIM))

    tok = lambda w, dt: jax.ShapeDtypeStruct((b, t, w), dt)
    tok_spec = lambda w: pl.BlockSpec((1, tm, w), lambda i, j: (i, j, 0))
    feat = jax.ShapeDtypeStruct((b, KV_WIDTH, t), BF16)
    feat_spec = pl.BlockSpec((1, KV_WIDTH, tm), lambda i, j: (i, 0, j))
    return pl.pallas_call(
        functools.partial(_odd_proj_kernel, tm=tm),
        grid=(b, t // tm),
        in_specs=[
            tok_spec(d),
            _const_spec((1, d)),
            _const_spec((d, ODD_COLS)),
            pl.BlockSpec((tm, LANES), lambda i, j: (j, 0)),
            pl.BlockSpec((tm, LANES), lambda i, j: (j, 0)),
            _const_spec((1, LANES)),
            _const_spec((2, LANES)),
            _const_spec((LANES, LANES)),
            _const_spec((CONF_K, CONF_WIDTH)),
            _const_spec((1, CONF_WIDTH)),
            _const_spec((1, CONF_WIDTH)),
            _const_spec((1, CONF_WIDTH)),
        ],
        out_specs=([tok_spec(NSA_WIDTH)] + [tok_spec(KV_WIDTH)] * 4 + [feat_spec, tok_spec(KV_WIDTH), feat_spec]
                   + [tok_spec(GATE_PAD), tok_spec(CONF_WIDTH)]),
        out_shape=([tok(NSA_WIDTH, BF16)] + [tok(KV_WIDTH, BF16)] * 4 + [feat, tok(KV_WIDTH, BF16), feat]
                   + [tok(GATE_PAD, F32), tok(CONF_WIDTH, BF16)]),
        scratch_shapes=[pltpu.VMEM((tm + 32, CONF_WIDTH), F32),
                        pltpu.VMEM((SUBLANES - 1, tm + 32 - SUBLANES, CONF_WIDTH), F32)],
        compiler_params=pltpu.CompilerParams(
            dimension_semantics=("arbitrary", "arbitrary"), vmem_limit_bytes=VMEM_LIMIT),
        name="odd_proj",
    )(x, g.reshape(1, d), w_all, cos, sin_signed, qn, kn, ones_bd,
      conf_dw, conf_dw_b.reshape(1, -1), conf_ln_g.reshape(1, -1), conf_ln_b.reshape(1, -1))


def _gelu_tanh(x):
    return 0.5 * x * (1.0 + jnp.tanh(np.sqrt(2.0 / np.pi) * (x + 0.044715 * (x * x * x))))


def _compress_kernel(kt_ref, vt_ref, w1_ref, pos_ref, w2_ref, kn_ref, kc_ref, vc_ref, sh_buf, *, n_rows):
    hw = N_GROUPS * CMP_HIDDEN
    outs = []
    for s, tok_ref in enumerate((kt_ref, vt_ref)):
        rows = tok_ref[0]
        first = _dot(rows, w1_ref[s, 0])
        second = _dot(rows, w1_ref[s, 1])
        sh_buf[0:n_rows, :] = second
        sh_buf[n_rows:n_rows + 8, :] = jnp.zeros((8, hw), F32)
        bias = _dot(pos_ref[s, 0], w1_ref[s, 0]) + _dot(pos_ref[s, 1], w1_ref[s, 1])
        hid = first + sh_buf[1:n_rows + 1, :] + bias[0:1, :]
        act = _gelu_tanh(hid).astype(BF16)
        per_group = [_dot(act[:, gi * CMP_HIDDEN:(gi + 1) * CMP_HIDDEN], w2_ref[s]) for gi in range(N_GROUPS)]
        outs.append(per_group)
    kc = [_rms_rows(o, kn_ref[...]) for o in outs[0]]
    kc_ref[0] = jnp.concatenate(kc, axis=-1).astype(BF16)
    vc_ref[0] = jnp.concatenate(outs[1], axis=-1).T.astype(BF16)


def _compress(kc_tok, vc_tok, cmp_pos, cmp_w1, cmp_w2, k_norm2):
    b, t, _ = kc_tok.shape
    n_rows = t // CMP_STRIDE
    row_w = CMP_STRIDE * KV_WIDTH
    hw = N_GROUPS * CMP_HIDDEN
    w1 = cmp_w1.reshape(2, 2, CMP_STRIDE, HEAD_DIM, CMP_HIDDEN)
    eye = jnp.eye(N_GROUPS, dtype=cmp_w1.dtype)
    w1x = jnp.einsum('shrdc,gk->shrgdkc', w1, eye).reshape(2, 2, row_w, hw).astype(BF16)
    pos = cmp_pos.reshape(2, 2, CMP_STRIDE, 1, HEAD_DIM)
    posx = jnp.broadcast_to(pos, (2, 2, CMP_STRIDE, N_GROUPS, HEAD_DIM)).reshape(2, 2, 1, row_w)
    posx = jnp.broadcast_to(posx, (2, 2, 8, row_w)).astype(BF16)
    out = jax.ShapeDtypeStruct((b, n_rows, KV_WIDTH), BF16)
    return pl.pallas_call(
        functools.partial(_compress_kernel, n_rows=n_rows),
        grid=(b,),
        in_specs=[
            pl.BlockSpec((1, n_rows, row_w), lambda i: (i, 0, 0)),
            pl.BlockSpec((1, n_rows, row_w), lambda i: (i, 0, 0)),
            _const_spec((2, 2, row_w, hw)),
            _const_spec((2, 2, 8, row_w)),
            _const_spec((2, CMP_HIDDEN, HEAD_DIM)),
            _const_spec((1, HEAD_DIM)),
        ],
        out_specs=[pl.BlockSpec((1, n_rows, KV_WIDTH), lambda i: (i, 0, 0)),
                   pl.BlockSpec((1, KV_WIDTH, n_rows), lambda i: (i, 0, 0))],
        out_shape=[out, jax.ShapeDtypeStruct((b, KV_WIDTH, n_rows), BF16)],
        scratch_shapes=[pltpu.VMEM((n_rows + 8, hw), F32)],
        compiler_params=pltpu.CompilerParams(
            dimension_semantics=("arbitrary",), vmem_limit_bytes=VMEM_LIMIT),
        name="compress",
    )(kc_tok.reshape(b, n_rows, row_w), vc_tok.reshape(b, n_rows, row_w), w1x, posx,
      cmp_w2.astype(BF16), k_norm2.reshape(1, HEAD_DIM))


def _per_head(bias):
    return jnp.concatenate([bias] * GROUP_SIZE, axis=1)


def _exp_keys(s):
    m = jnp.maximum(jnp.max(s, axis=0, keepdims=True), M_FLOOR)
    e = jnp.exp2(s - m)
    return e, jnp.sum(e, axis=0, keepdims=True)


def _nsa_kernel(q_ref, gt_ref, kc_ref, vct_ref, ksa_ref, ksb_ref, vst_ref, kw_ref, vwt_ref, ovt_ref, o_ref,
                *, t_len, ck):
    qb = pl.program_id(1)
    q0 = qb * Q_BLOCK
    cols = GROUP_SIZE * Q_BLOCK
    n_cmp_pad = kc_ref.shape[1]
    n_sb = t_len // SEL_BLOCK
    n_sel = min(N_SELECT, n_sb)
    win_len = min(WINDOW + Q_BLOCK, t_len)

    q_all = q_ref[0]
    lane = lax.broadcasted_iota(jnp.int32, (1, LANES), 1)
    low = lane < HEAD_DIM
    pos_q = q0 + lane
    row_id = lambda n: lax.broadcasted_iota(jnp.int32, (n, 1), 0)

    cmp_end = row_id(n_cmp_pad) * CMP_STRIDE + (CMP_BLOCK - 1)
    bias_c = _per_head(jnp.where(cmp_end <= pos_q, 0.0, NEG))
    w0 = pl.multiple_of(jnp.maximum(q0 + Q_BLOCK - win_len, 0), Q_BLOCK)
    kpos_w = w0 + row_id(win_len)
    bias_w = _per_head(jnp.where((kpos_w <= pos_q) & (kpos_w > pos_q - WINDOW), 0.0, NEG))
    n_full = q0 // ck
    k_tail = pl.multiple_of(n_full * ck, ck)
    bias_t = _per_head(jnp.where(k_tail + row_id(ck) <= pos_q, 0.0, NEG))

    ks_refs = (ksa_ref, ksb_ref)
    q_sel = []
    out_c, out_w = [], []
    for g in range(N_GROUPS):
        is_q = low if g == 0 else jnp.logical_not(low)
        q_tiles = [q_all[:, j * LANES:(j + 1) * LANES] for j in range(GROUP_SIZE)]
        qz = jnp.concatenate([jnp.where(is_q, qt, jnp.zeros((), BF16)) for qt in q_tiles], axis=0)

        e_c, l_c = _exp_keys(_dot_nt(kc_ref[0], qz) + bias_c)
        p_c = e_c * (1.0 / jnp.maximum(l_c, TINY))
        out_c.append(_dot(vct_ref[0], p_c.astype(BF16)))

        p_sum = p_c[:, 0:Q_BLOCK]
        for j in range(1, GROUP_SIZE):
            p_sum = p_sum + p_c[:, j * Q_BLOCK:(j + 1) * Q_BLOCK]
        hi = p_sum.astype(BF16)
        lo = (p_sum - hi.astype(F32)).astype(BF16)
        imp = _dot(ovt_ref[g], hi) + _dot(ovt_ref[g], lo)
        jj = row_id(LANES) - (HEAD_DIM if g == 0 else 0)
        in_blk = (jj >= 0) & (jj < n_sb)
        cur = pos_q // SEL_BLOCK
        forced = in_blk & ((jj == 0) | (jj == cur) | (jj == cur - 1))
        score = jnp.where(jj * SEL_BLOCK <= pos_q, imp, -1.0)
        score = jnp.where(in_blk & jnp.logical_not(forced), score, -jnp.inf)
        not_sel = jnp.where(forced, 0.0, 1.0)
        jjf = jj.astype(F32)
        for _ in range(n_sel - N_FORCED):
            m = jnp.max(score, axis=0, keepdims=True)
            first = jnp.min(jnp.where(score == m, jjf, float(LANES)), axis=0, keepdims=True)
            pick = jjf == first
            not_sel = jnp.where(pick, 0.0, not_sel)
            score = jnp.where(pick, -jnp.inf, score)
        not_sel = not_sel.T.astype(BF16)
        q_sel.append(jnp.concatenate([jnp.where(is_q, qt, not_sel) for qt in q_tiles], axis=0))

        e_w, l_w = _exp_keys(_dot_nt(kw_ref[0, pl.ds(w0, win_len), :], qz) + bias_w)
        o_w = _dot(vwt_ref[0, :, pl.ds(w0, win_len)], e_w.astype(BF16))
        out_w.append(o_w * (1.0 / jnp.maximum(l_w, TINY)))

    def sel_chunk(state, k0, bias):
        new = []
        for g in range(N_GROUPS):
            m_run, l_run, acc = state[g]
            s = _dot_nt(ks_refs[g][0, pl.ds(k0, ck), :], q_sel[g])
            if bias is not None:
                s = s + bias
            m_new = jnp.maximum(m_run, jnp.max(s, axis=0, keepdims=True))
            alpha = jnp.exp2(m_run - m_new)
            e = jnp.exp2(s - m_new)
            l_new = alpha * l_run + jnp.sum(e, axis=0, keepdims=True)
            acc = alpha * acc + _dot(vst_ref[0, :, pl.ds(k0, ck)], e.astype(BF16))
            new.append((m_new, l_new, acc))
        return tuple(new)

    one = (jnp.full((1, cols), M_FLOOR, F32), jnp.zeros((1, cols), F32), jnp.zeros((LANES, cols), F32))
    state = lax.fori_loop(
        0, n_full, lambda c, st: sel_chunk(st, pl.multiple_of(c * ck, ck), None), (one,) * N_GROUPS)
    state = sel_chunk(state, k_tail, bias_t)
    out_s = [acc * (1.0 / jnp.maximum(l_run, TINY)) for _, l_run, acc in state]

    gt = gt_ref[0].T
    total = []
    for g in range(N_GROUPS):
        acc = None
        for br, outs in enumerate((out_c, out_s, out_w)):
            rows = [3 * (g * GROUP_SIZE + j) + br for j in range(GROUP_SIZE)]
            gate = jnp.concatenate([gt[r:r + 1, :] for r in rows], axis=1)
            term = gate * outs[g]
            acc = term if acc is None else acc + term
        total.append(acc)
    merged = jnp.where(row_id(LANES) < HEAD_DIM, total[0], total[1])
    o_ref[0] = jnp.concatenate(
        [merged[:, j * Q_BLOCK:(j + 1) * Q_BLOCK].T for j in range(GROUP_SIZE)], axis=1).astype(BF16)


def _nsa_attention(q, gates, kc, vct, ksa, ksb, vst, kw, vwt, *, ck):
    b, t, _ = q.shape
    n_cmp_pad = kc.shape[1]
    n_sb = t // SEL_BLOCK
    assert Q_BLOCK == LANES
    assert N_FORCED <= min(N_SELECT, n_sb) and n_sb <= HEAD_DIM, "selection blocks must fit the 64 spare lanes"
    ck = min(ck, t)
    ci = np.arange(n_cmp_pad)[None, :] * CMP_STRIDE
    sj = np.arange(n_sb)[:, None] * SEL_BLOCK
    ov = (ci < sj + SEL_BLOCK) & (ci + CMP_BLOCK - 1 >= sj)
    overlap = np.zeros((N_GROUPS, LANES, n_cmp_pad), np.float32)
    overlap[0, HEAD_DIM:HEAD_DIM + n_sb] = ov
    overlap[1, 0:n_sb] = ov
    overlap = jnp.asarray(overlap, BF16)

    tok_spec = pl.BlockSpec((1, t, KV_WIDTH), lambda i, j: (i, 0, 0))
    feat_spec = pl.BlockSpec((1, KV_WIDTH, t), lambda i, j: (i, 0, 0))
    return pl.pallas_call(
        functools.partial(_nsa_kernel, t_len=t, ck=ck),
        grid=(b, t // Q_BLOCK),
        in_specs=[
            pl.BlockSpec((1, Q_BLOCK, NSA_WIDTH), lambda i, j: (i, j, 0)),
            pl.BlockSpec((1, Q_BLOCK, GATE_PAD), lambda i, j: (i, j, 0)),
            pl.BlockSpec((1, n_cmp_pad, KV_WIDTH), lambda i, j: (i, 0, 0)),
            pl.BlockSpec((1, KV_WIDTH, n_cmp_pad), lambda i, j: (i, 0, 0)),
            tok_spec, tok_spec, feat_spec, tok_spec, feat_spec,
            _const_spec((N_GROUPS, LANES, n_cmp_pad)),
        ],
        out_specs=pl.BlockSpec((1, Q_BLOCK, NSA_WIDTH), lambda i, j: (i, j, 0)),
        out_shape=jax.ShapeDtypeStruct((b, t, NSA_WIDTH), BF16),
        compiler_params=pltpu.CompilerParams(
            dimension_semantics=("arbitrary", "arbitrary"), vmem_limit_bytes=VMEM_LIMIT),
        name="nsa_attention",
    )(q, gates, kc, vct, ksa, ksb, vst, kw, vwt, overlap)


def _tile(t, want):
    return want if t % want == 0 else t


def kernel(x, norm_mix, norm_ffn, ffn_gate, ffn_up, ffn_down, w_in_even, conv_a, pool_w, pool_scale,
           w_out_even, w_in_odd, q_norm, k_norm, cmp_pos, cmp_w1, cmp_w2, conf_dw, conf_dw_b,
           conf_ln_g, conf_ln_b, w_out_odd):
    depth = norm_mix.shape[0]
    t = x.shape[1]
    tm = _tile(t, 512)
    d_ff = ffn_gate.shape[2]
    chunk = 256 if d_ff % 256 == 0 else d_ff
    for i in range(depth):
        if i % 2 == 0:
            e = i // 2
            a, z = _even_mixer(x, norm_mix[i], w_in_even[e], conv_a[e], pool_w[e], pool_scale[e], tm=tm)
            w_out = w_out_even[e]
        else:
            o = i // 2
            q, kc_tok, vc_tok, ksa, ksb, vs, kw, vw, gates, z = _odd_proj(
                x, norm_mix[i], w_in_odd[o], q_norm[o], k_norm[o], conf_dw[o], conf_dw_b[o],
                conf_ln_g[o], conf_ln_b[o], tm=tm)
            kc, vc = _compress(kc_tok, vc_tok, cmp_pos[o], cmp_w1[o], cmp_w2[o], k_norm[o, 2])
            a = _nsa_attention(q, gates, kc, vc, ksa, ksb, vs, kw, vw, ck=512)
            w_out = jnp.concatenate([w_out_odd[o][:NSA_WIDTH][_q_perm()], w_out_odd[o][NSA_WIDTH:]], axis=0)
        x = _ffn(x, a, z, w_out, norm_ffn[i], ffn_gate[i], ffn_up[i], ffn_down[i], tm=tm, chunk=chunk)
    return x
```

```python
import functools

import numpy as np
import jax
import jax.numpy as jnp
from jax import lax
from jax.experimental import pallas as pl
from jax.experimental.pallas import tpu as pltpu

EPS = 1e-6
ROPE_THETA = 10000.0
HEAD_DIM = 64
HALF = HEAD_DIM // 2
LANES = 128
SUBLANES = 8
LOG2E = 1.4426950408889634
N_HEADS = 8
N_GROUPS = 2
GROUP_SIZE = N_HEADS // N_GROUPS
NSA_WIDTH = N_HEADS * HEAD_DIM
KV_WIDTH = N_GROUPS * HEAD_DIM
CONF_WIDTH = 512
CONF_K = 31
CONV_WIDTH = 512
POOL_WINDOWS = (2, 4, 8, 16)
POOL_GROUP = 128
SHORT_CONV_K = 3
CMP_BLOCK = 32
CMP_STRIDE = 16
CMP_HIDDEN = 4 * HEAD_DIM
SEL_BLOCK = 64
N_SELECT = 8
N_FORCED = 3
WINDOW = 512
Q_BLOCK = 128
NEG = -1e30
M_FLOOR = -1e29
TINY = 1e-30
GATE_PAD = LANES
ODD_COLS = NSA_WIDTH + 6 * KV_WIDTH + GATE_PAD + 2 * CONF_WIDTH

VMEM_LIMIT = 56 * 1024 * 1024

BF16 = jnp.bfloat16
F32 = jnp.float32


def _dot(a, b):
    return jnp.dot(a, b, preferred_element_type=F32)


def _dot_nt(a, b):
    return lax.dot_general(a, b, (((1,), (1,)), ((), ())), preferred_element_type=F32)


def _split_dot(a, b):
    hi = a.astype(BF16)
    lo = (a - hi.astype(F32)).astype(BF16)
    return _dot(hi, b) + _dot(lo, b)


def _rms_rows(x, g):
    ms = jnp.mean(x * x, axis=-1, keepdims=True)
    return x * lax.rsqrt(ms + EPS) * g


def _sigmoid(x):
    return 1.0 / (1.0 + jnp.exp(-x))


def _const_spec(shape):
    nd = len(shape)
    return pl.BlockSpec(shape, lambda *_: (0,) * nd, pipeline_mode=pl.Buffered(1))


def _ffn_kernel(x_ref, a_ref, z_ref, wa_ref, wz_ref, g_ref, wg_ref, wu_ref, wd_ref, o_ref, x1_buf, *, chunk):
    x1_buf[...] = x_ref[0] + (_dot(a_ref[0], wa_ref[...]) + _dot(z_ref[0], wz_ref[...]))
    x = x1_buf[...]
    h = _rms_rows(x, g_ref[...]).astype(BF16)
    acc = x
    for c0 in range(0, wg_ref.shape[1], chunk):
        gate = _dot(h, wg_ref[:, c0:c0 + chunk])
        up = _dot(h, wu_ref[:, c0:c0 + chunk])
        a = (gate * _sigmoid(gate) * up).astype(BF16)
        acc = acc + _dot(a, wd_ref[c0:c0 + chunk, :])
    o_ref[0] = acc


def _ffn(x, a, z, w_out, g, wg, wu, wd, *, tm, chunk):
    b, t, d = x.shape
    wa_n, wz_n = a.shape[2], z.shape[2]
    dff = wg.shape[1]
    return pl.pallas_call(
        functools.partial(_ffn_kernel, chunk=chunk),
        grid=(b, t // tm),
        in_specs=[
            pl.BlockSpec((1, tm, d), lambda i, j: (i, j, 0)),
            pl.BlockSpec((1, tm, wa_n), lambda i, j: (i, j, 0)),
            pl.BlockSpec((1, tm, wz_n), lambda i, j: (i, j, 0)),
            _const_spec((wa_n, d)),
            _const_spec((wz_n, d)),
            _const_spec((1, d)),
            _const_spec((d, dff)),
            _const_spec((d, dff)),
            _const_spec((dff, d)),
        ],
        out_specs=pl.BlockSpec((1, tm, d), lambda i, j: (i, j, 0)),
        out_shape=jax.ShapeDtypeStruct(x.shape, F32),
        scratch_shapes=[pltpu.VMEM((tm, d), F32)],
        compiler_params=pltpu.CompilerParams(
            dimension_semantics=("arbitrary", "arbitrary"), vmem_limit_bytes=VMEM_LIMIT),
        name="ffn",
    )(x, a, z, w_out[:wa_n].astype(BF16), w_out[wa_n:].astype(BF16), g.reshape(1, d),
      wg.astype(BF16), wu.astype(BF16), wd.astype(BF16))


def _even_kernel(x_ref, g_ref, win_ref, cw_ref, pw_ref, ps_ref, yc_ref, yp_ref,
                 cv_buf, vp_buf, *, tm):
    t = pl.program_id(1)
    x = x_ref[0]
    h = _rms_rows(x, g_ref[...]).astype(BF16)
    u = _dot(h, win_ref[...])
    b_gate = u[:, 0:CONV_WIDTH]
    cv = u[:, CONV_WIDTH:2 * CONV_WIDTH] * u[:, 2 * CONV_WIDTH:3 * CONV_WIDTH]
    vp = u[:, 3 * CONV_WIDTH:]

    @pl.when(t == 0)
    def _():
        cv_buf[0:8, :] = jnp.zeros((8, CONV_WIDTH), F32)
        vp_buf[0:16, :] = jnp.zeros((16, 4 * POOL_GROUP), F32)

    @pl.when(t > 0)
    def _():
        cv_buf[0:8, :] = cv_buf[tm:tm + 8, :]
        vp_buf[0:16, :] = vp_buf[tm:tm + 16, :]

    cv_buf[8:8 + tm, :] = cv
    vp_buf[16:16 + tm, :] = vp

    cw = cw_ref[...]
    conv = cw[2:3, :] * cv
    for k in range(SHORT_CONV_K - 1):
        off = 8 - (SHORT_CONV_K - 1) + k
        conv = conv + cw[k:k + 1, :] * cv_buf[off:off + tm, :]
    yc_ref[0] = (b_gate * conv).astype(BF16)

    pos = t * tm + lax.broadcasted_iota(jnp.int32, (tm, 1), 0)
    ys = []
    for gi, w in enumerate(POOL_WINDOWS):
        lo, hi = gi * POOL_GROUP, (gi + 1) * POOL_GROUP
        s = vp[:, lo:hi]
        for k in range(1, w):
            s = s + vp_buf[16 - k:16 - k + tm, lo:hi]
        cnt = jnp.minimum(pos + 1, w).astype(F32)
        pooled = s / cnt - vp[:, lo:hi]
        ys.append(_dot(pooled.astype(BF16), pw_ref[gi]))
    yp_ref[0] = (jnp.concatenate(ys, axis=-1) * ps_ref[...]).astype(BF16)


def _even_mixer(x, g, w_in, conv_w, pool_w, pool_scale, *, tm):
    b, t, d = x.shape
    n_in = w_in.shape[1]
    return pl.pallas_call(
        functools.partial(_even_kernel, tm=tm),
        grid=(b, t // tm),
        in_specs=[
            pl.BlockSpec((1, tm, d), lambda i, j: (i, j, 0)),
            _const_spec((1, d)),
            _const_spec((d, n_in)),
            _const_spec((SHORT_CONV_K, CONV_WIDTH)),
            _const_spec((len(POOL_WINDOWS), POOL_GROUP, POOL_GROUP)),
            _const_spec((1, 4 * POOL_GROUP)),
        ],
        out_specs=[pl.BlockSpec((1, tm, CONV_WIDTH), lambda i, j: (i, j, 0)),
                   pl.BlockSpec((1, tm, 4 * POOL_GROUP), lambda i, j: (i, j, 0))],
        out_shape=[jax.ShapeDtypeStruct((b, t, CONV_WIDTH), BF16),
                   jax.ShapeDtypeStruct((b, t, 4 * POOL_GROUP), BF16)],
        scratch_shapes=[pltpu.VMEM((tm + 8, CONV_WIDTH), F32),
                        pltpu.VMEM((tm + 16, 4 * POOL_GROUP), F32)],
        compiler_params=pltpu.CompilerParams(
            dimension_semantics=("arbitrary", "arbitrary"), vmem_limit_bytes=VMEM_LIMIT),
        name="even_mixer",
    )(x, g.reshape(1, d), w_in.astype(BF16), conv_w, pool_w.astype(BF16), pool_scale.reshape(1, -1))


def _rope_tile(x, cos, sin_signed, first_half):
    rot = jnp.where(first_half, pltpu.roll(x, LANES - HALF, 1), pltpu.roll(x, HALF, 1))
    return x * cos + rot * sin_signed


def _head_ms(x, ones_bd):
    return _split_dot(x * x, ones_bd) * (1.0 / HEAD_DIM)


def _odd_proj_kernel(x_ref, g_ref, win_ref, cos_ref, sin_ref, qn_ref, kn_ref, bd_ref,
                     dw_ref, dwb_ref, lng_ref, lnb_ref,
                     q_ref, kc_ref, vc_ref, ksa_ref, ksb_ref, vs_ref, kw_ref, vw_ref, gt_ref, z_ref,
                     z_buf, ph_buf, *, tm):
    t = pl.program_id(1)
    x = x_ref[0]
    h = _rms_rows(x, g_ref[...]).astype(BF16)
    u = _dot(h, win_ref[...])
    cos = cos_ref[...]
    sin = sin_ref[...]
    lane = lax.broadcasted_iota(jnp.int32, (1, LANES), 1)
    first_half = (lane % HEAD_DIM) < HALF
    bd = bd_ref[...]

    scale = LOG2E * HEAD_DIM ** -0.5
    q_parts = []
    for j in range(NSA_WIDTH // LANES):
        qj = u[:, j * LANES:(j + 1) * LANES]
        qj = qj * lax.rsqrt(_head_ms(qj, bd) + EPS) * qn_ref[...]
        q_parts.append(_rope_tile(qj, cos, sin, first_half) * scale)
    q_ref[0] = jnp.concatenate(q_parts, axis=-1).astype(BF16)

    o = NSA_WIDTH
    kc_ref[0] = _rope_tile(u[:, o:o + KV_WIDTH], cos, sin, first_half).astype(BF16)
    o += KV_WIDTH
    vc_ref[0] = u[:, o:o + KV_WIDTH].astype(BF16)
    o += KV_WIDTH
    k = u[:, o:o + KV_WIDTH]
    k = k * lax.rsqrt(_head_ms(k, bd) + EPS) * kn_ref[0:1, :]
    k = _rope_tile(k, cos, sin, first_half)
    blk = (t * tm + lax.broadcasted_iota(jnp.int32, (tm, 1), 0)) // SEL_BLOCK
    low = lane < HEAD_DIM
    ksa_ref[0] = jnp.where(low, k, jnp.where(lane - HEAD_DIM == blk, NEG, 0.0)).astype(BF16)
    ksb_ref[0] = jnp.where(low, jnp.where(lane == blk, NEG, 0.0), k).astype(BF16)
    o += KV_WIDTH
    vs_ref[0] = u[:, o:o + KV_WIDTH].T.astype(BF16)
    o += KV_WIDTH
    k = u[:, o:o + KV_WIDTH]
    k = k * lax.rsqrt(_head_ms(k, bd) + EPS) * kn_ref[1:2, :]
    kw_ref[0] = _rope_tile(k, cos, sin, first_half).astype(BF16)
    o += KV_WIDTH
    vw_ref[0] = u[:, o:o + KV_WIDTH].T.astype(BF16)
    o += KV_WIDTH
    gt_ref[0] = _sigmoid(u[:, o:o + GATE_PAD])
    o += GATE_PAD

    zz = u[:, o:o + CONF_WIDTH] * _sigmoid(u[:, o + CONF_WIDTH:o + 2 * CONF_WIDTH])
    halo = 32

    @pl.when(t == 0)
    def _():
        z_buf[0:halo, :] = jnp.zeros((halo, CONF_WIDTH), F32)

    @pl.when(t > 0)
    def _():
        z_buf[0:halo, :] = z_buf[tm:tm + halo, :]

    z_buf[halo:halo + tm, :] = zz
    for r in range(1, SUBLANES):
        ph_buf[r - 1] = z_buf[r:r + tm + halo - SUBLANES, :]
    dw = dw_ref[...]
    conv = dw[CONF_K - 1:CONF_K, :] * zz + dwb_ref[...]
    for kk in range(CONF_K - 1):
        off = halo - (CONF_K - 1) + kk
        a, r = off // SUBLANES * SUBLANES, off % SUBLANES
        tap = z_buf[a:a + tm, :] if r == 0 else ph_buf[r - 1, a:a + tm, :]
        conv = conv + dw[kk:kk + 1, :] * tap
    mu = jnp.mean(conv, axis=-1, keepdims=True)
    cen = conv - mu
    var = jnp.mean(cen * cen, axis=-1, keepdims=True)
    y = cen * lax.rsqrt(var + EPS) * lng_ref[...] + lnb_ref[...]
    z_ref[0] = (y * _sigmoid(y)).astype(BF16)


def _q_perm():
    p = np.arange(NSA_WIDTH)
    j, within = p // LANES, p % LANES
    head = np.where(within < HEAD_DIM, j, GROUP_SIZE + j)
    return head * HEAD_DIM + within % HEAD_DIM


def _odd_proj(x, g, w_in, q_norm, k_norm, conf_dw, conf_dw_b, conf_ln_g, conf_ln_b, *, tm):
    b, t, d = x.shape
    perm = _q_perm()
    n_kv = 6 * KV_WIDTH
    n_gate = 3 * N_HEADS
    w_q = w_in[:, :NSA_WIDTH][:, perm]
    w_kv = w_in[:, NSA_WIDTH:NSA_WIDTH + n_kv]
    w_gt = jnp.pad(w_in[:, NSA_WIDTH + n_kv:NSA_WIDTH + n_kv + n_gate], ((0, 0), (0, GATE_PAD - n_gate)))
    w_cf = w_in[:, NSA_WIDTH + n_kv + n_gate:]
    w_all = jnp.concatenate([w_q, w_kv, w_gt, w_cf], axis=1).astype(BF16)

    inv = 1.0 / (ROPE_THETA ** (jnp.arange(0, HEAD_DIM, 2, dtype=F32) / HEAD_DIM))
    ang = jnp.arange(t, dtype=F32)[:, None] * inv[None, :]
    cos = jnp.tile(jnp.cos(ang), (1, LANES // HALF))
    sin = jnp.tile(jnp.sin(ang), (1, LANES // HALF))
    lane = np.arange(LANES)
    sin_signed = jnp.where((lane % HEAD_DIM < HALF)[None, :], -sin, sin)

    ones_bd = jnp.asarray((lane[:, None] // HEAD_DIM == lane[None, :] // HEAD_DIM), BF16)
    qn = jnp.tile(q_norm, LANES // HEAD_DIM).reshape(1, LANES)
    kn = jnp.tile(k_norm[:2], (1, LANES // HEAD_DIM))

    tok = lambda w, dt: jax.ShapeDtypeStruct((b, t, w), dt)
    tok_spec = lambda w: pl.BlockSpec((1, tm, w), lambda i, j: (i, j, 0))
    feat = jax.ShapeDtypeStruct((b, KV_WIDTH, t), BF16)
    feat_spec = pl.BlockSpec((1, KV_WIDTH, tm), lambda i, j: (i, 0, j))
    return pl.pallas_call(
        functools.partial(_odd_proj_kernel, tm=tm),
        grid=(b, t // tm),
        in_specs=[
            tok_spec(d),
            _const_spec((1, d)),
            _const_spec((d, ODD_COLS)),
            pl.BlockSpec((tm, LANES), lambda i, j: (j, 0)),
            pl.BlockSpec((tm, LANES), lambda i, j: (j, 0)),
            _const_spec((1, LANES)),
            _const_spec((2, LANES)),
            _const_spec((LANES, LANES)),
            _const_spec((CONF_K, CONF_WIDTH)),
            _const_spec((1, CONF_WIDTH)),
            _const_spec((1, CONF_WIDTH)),
            _const_spec((1, CONF_WIDTH)),
        ],
        out_specs=([tok_spec(NSA_WIDTH)] + [tok_spec(KV_WIDTH)] * 4 + [feat_spec, tok_spec(KV_WIDTH), feat_spec]
                   + [tok_spec(GATE_PAD), tok_spec(CONF_WIDTH)]),
        out_shape=([tok(NSA_WIDTH, BF16)] + [tok(KV_WIDTH, BF16)] * 4 + [feat, tok(KV_WIDTH, BF16), feat]
                   + [tok(GATE_PAD, F32), tok(CONF_WIDTH, BF16)]),
        scratch_shapes=[pltpu.VMEM((tm + 32, CONF_WIDTH), F32),
                        pltpu.VMEM((SUBLANES - 1, tm + 32 - SUBLANES, CONF_WIDTH), F32)],
        compiler_params=pltpu.CompilerParams(
            dimension_semantics=("arbitrary", "arbitrary"), vmem_limit_bytes=VMEM_LIMIT),
        name="odd_proj",
    )(x, g.reshape(1, d), w_all, cos, sin_signed, qn, kn, ones_bd,
      conf_dw, conf_dw_b.reshape(1, -1), conf_ln_g.reshape(1, -1), conf_ln_b.reshape(1, -1))


def _gelu_tanh(x):
    return 0.5 * x * (1.0 + jnp.tanh(np.sqrt(2.0 / np.pi) * (x + 0.044715 * (x * x * x))))


def _compress_kernel(kt_ref, vt_ref, w1_ref, pos_ref, w2_ref, kn_ref, kc_ref, vc_ref, sh_buf, *, n_rows):
    hw = N_GROUPS * CMP_HIDDEN
    outs = []
    for s, tok_ref in enumerate((kt_ref, vt_ref)):
        rows = tok_ref[0]
        first = _dot(rows, w1_ref[s, 0])
        second = _dot(rows, w1_ref[s, 1])
        sh_buf[0:n_rows, :] = second
        sh_buf[n_rows:n_rows + 8, :] = jnp.zeros((8, hw), F32)
        bias = _dot(pos_ref[s, 0], w1_ref[s, 0]) + _dot(pos_ref[s, 1], w1_ref[s, 1])
        hid = first + sh_buf[1:n_rows + 1, :] + bias[0:1, :]
        act = _gelu_tanh(hid).astype(BF16)
        per_group = [_dot(act[:, gi * CMP_HIDDEN:(gi + 1) * CMP_HIDDEN], w2_ref[s]) for gi in range(N_GROUPS)]
        outs.append(per_group)
    kc = [_rms_rows(o, kn_ref[...]) for o in outs[0]]
    kc_ref[0] = jnp.concatenate(kc, axis=-1).astype(BF16)
    vc_ref[0] = jnp.concatenate(outs[1], axis=-1).T.astype(BF16)


def _compress(kc_tok, vc_tok, cmp_pos, cmp_w1, cmp_w2, k_norm2):
    b, t, _ = kc_tok.shape
    n_rows = t // CMP_STRIDE
    row_w = CMP_STRIDE * KV_WIDTH
    hw = N_GROUPS * CMP_HIDDEN
    w1 = cmp_w1.reshape(2, 2, CMP_STRIDE, HEAD_DIM, CMP_HIDDEN)
    eye = jnp.eye(N_GROUPS, dtype=cmp_w1.dtype)
    w1x = jnp.einsum('shrdc,gk->shrgdkc', w1, eye).reshape(2, 2, row_w, hw).astype(BF16)
    pos = cmp_pos.reshape(2, 2, CMP_STRIDE, 1, HEAD_DIM)
    posx = jnp.broadcast_to(pos, (2, 2, CMP_STRIDE, N_GROUPS, HEAD_DIM)).reshape(2, 2, 1, row_w)
    posx = jnp.broadcast_to(posx, (2, 2, 8, row_w)).astype(BF16)
    out = jax.ShapeDtypeStruct((b, n_rows, KV_WIDTH), BF16)
    return pl.pallas_call(
        functools.partial(_compress_kernel, n_rows=n_rows),
        grid=(b,),
        in_specs=[
            pl.BlockSpec((1, n_rows, row_w), lambda i: (i, 0, 0)),
            pl.BlockSpec((1, n_rows, row_w), lambda i: (i, 0, 0)),
            _const_spec((2, 2, row_w, hw)),
            _const_spec((2, 2, 8, row_w)),
            _const_spec((2, CMP_HIDDEN, HEAD_DIM)),
            _const_spec((1, HEAD_DIM)),
        ],
        out_specs=[pl.BlockSpec((1, n_rows, KV_WIDTH), lambda i: (i, 0, 0)),
                   pl.BlockSpec((1, KV_WIDTH, n_rows), lambda i: (i, 0, 0))],
        out_shape=[out, jax.ShapeDtypeStruct((b, KV_WIDTH, n_rows), BF16)],
        scratch_shapes=[pltpu.VMEM((n_rows + 8, hw), F32)],
        compiler_params=pltpu.CompilerParams(
            dimension_semantics=("arbitrary",), vmem_limit_bytes=VMEM_LIMIT),
        name="compress",
    )(kc_tok.reshape(b, n_rows, row_w), vc_tok.reshape(b, n_rows, row_w), w1x, posx,
      cmp_w2.astype(BF16), k_norm2.reshape(1, HEAD_DIM))


def _per_head(bias):
    return jnp.concatenate([bias] * GROUP_SIZE, axis=1)


def _exp_keys(s):
    m = jnp.maximum(jnp.max(s, axis=0, keepdims=True), M_FLOOR)
    e = jnp.exp2(s - m)
    return e, jnp.sum(e, axis=0, keepdims=True)


def _nsa_kernel(q_ref, gt_ref, kc_ref, vct_ref, ksa_ref, ksb_ref, vst_ref, kw_ref, vwt_ref, ovt_ref, o_ref,
                s_a, s_b, s_w, *, t_len, ck):
    qb = pl.program_id(1)
    q0 = qb * Q_BLOCK
    cols = GROUP_SIZE * Q_BLOCK
    n_cmp_pad = kc_ref.shape[1]
    n_sb = t_len // SEL_BLOCK
    n_sel = min(N_SELECT, n_sb)
    win_len = min(WINDOW + Q_BLOCK, t_len)

    q_all = q_ref[0]
    lane = lax.broadcasted_iota(jnp.int32, (1, LANES), 1)
    low = lane < HEAD_DIM
    pos_q = q0 + lane
    row_id = lambda n: lax.broadcasted_iota(jnp.int32, (n, 1), 0)

    cmp_end = row_id(n_cmp_pad) * CMP_STRIDE + (CMP_BLOCK - 1)
    bias_c = _per_head(jnp.where(cmp_end <= pos_q, 0.0, NEG))
    w0 = pl.multiple_of(jnp.maximum(q0 + Q_BLOCK - win_len, 0), Q_BLOCK)
    kpos_w = w0 + row_id(win_len)
    bias_w = _per_head(jnp.where((kpos_w <= pos_q) & (kpos_w > pos_q - WINDOW), 0.0, NEG))
    n_full = q0 // ck
    k_tail = pl.multiple_of(n_full * ck, ck)
    bias_t = _per_head(jnp.where(k_tail + row_id(ck) <= pos_q, 0.0, NEG))

    ks_refs = (ksa_ref, ksb_ref)
    q_tiles = [q_all[:, j * LANES:(j + 1) * LANES] for j in range(GROUP_SIZE)]
    is_q = (low, jnp.logical_not(low))
    qz = [jnp.concatenate([jnp.where(is_q[g], qt, jnp.zeros((), BF16)) for qt in q_tiles], axis=0)
          for g in range(N_GROUPS)]
    for g in range(N_GROUPS):
        s_w[g] = _dot_nt(kw_ref[0, pl.ds(w0, win_len), :], qz[g])
    q_sel = []
    out_c, out_w = [], []
    for g in range(N_GROUPS):

        e_c, l_c = _exp_keys(_dot_nt(kc_ref[0], qz[g]) + bias_c)
        p_c = e_c * (1.0 / jnp.maximum(l_c, TINY))
        out_c.append(_dot(vct_ref[0], p_c.astype(BF16)))

        p_sum = p_c[:, 0:Q_BLOCK]
        for j in range(1, GROUP_SIZE):
            p_sum = p_sum + p_c[:, j * Q_BLOCK:(j + 1) * Q_BLOCK]
        hi = p_sum.astype(BF16)
        lo = (p_sum - hi.astype(F32)).astype(BF16)
        imp = _dot(ovt_ref[g], hi) + _dot(ovt_ref[g], lo)
        jj = row_id(LANES) - (HEAD_DIM if g == 0 else 0)
        in_blk = (jj >= 0) & (jj < n_sb)
        cur = pos_q // SEL_BLOCK
        forced = in_blk & ((jj == 0) | (jj == cur) | (jj == cur - 1))
        score = jnp.where(jj * SEL_BLOCK <= pos_q, imp, -1.0)
        score = jnp.where(in_blk & jnp.logical_not(forced), score, -jnp.inf)
        not_sel = jnp.where(forced, 0.0, 1.0)
        jjf = jj.astype(F32)
        for _ in range(n_sel - N_FORCED):
            m = jnp.max(score, axis=0, keepdims=True)
            first = jnp.min(jnp.where(score == m, jjf, float(LANES)), axis=0, keepdims=True)
            pick = jjf == first
            not_sel = jnp.where(pick, 0.0, not_sel)
            score = jnp.where(pick, -jnp.inf, score)
        not_sel = not_sel.T.astype(BF16)
        q_sel.append(jnp.concatenate([jnp.where(is_q[g], qt, not_sel) for qt in q_tiles], axis=0))

        e_w, l_w = _exp_keys(s_w[g] + bias_w)
        o_w = _dot(vwt_ref[0, :, pl.ds(w0, win_len)], e_w.astype(BF16))
        out_w.append(o_w * (1.0 / jnp.maximum(l_w, TINY)))

    def sel_scores(buf, k0):
        for g in range(N_GROUPS):
            buf[g] = _dot_nt(ks_refs[g][0, pl.ds(k0, ck), :], q_sel[g])

    def sel_update(state, buf, k0, bias=None):
        new = []
        for g in range(N_GROUPS):
            m_run, l_run, acc = state[g]
            s = buf[g] if bias is None else buf[g] + bias
            m_new = jnp.maximum(m_run, jnp.max(s, axis=0, keepdims=True))
            alpha = jnp.exp2(m_run - m_new)
            e = jnp.exp2(s - m_new)
            l_new = alpha * l_run + jnp.sum(e, axis=0, keepdims=True)
            acc = alpha * acc + _dot(vst_ref[0, :, pl.ds(k0, ck)], e.astype(BF16))
            new.append((m_new, l_new, acc))
        return tuple(new)

    def pair_body(i, state):
        k0 = pl.multiple_of(2 * i * ck, ck)
        sel_scores(s_b, k0 + ck)
        state = sel_update(state, s_a, k0)
        sel_scores(s_a, k0 + 2 * ck)
        return sel_update(state, s_b, k0 + ck)

    one = (jnp.full((1, cols), M_FLOOR, F32), jnp.zeros((1, cols), F32), jnp.zeros((LANES, cols), F32))
    sel_scores(s_a, 0)
    state = lax.fori_loop(0, n_full // 2, pair_body, (one,) * N_GROUPS)

    def odd_rest(st):
        sel_scores(s_b, k_tail)
        st = sel_update(st, s_a, k_tail - ck)
        return sel_update(st, s_b, k_tail, bias_t)

    state = lax.cond(n_full % 2 == 1, odd_rest, lambda st: sel_update(st, s_a, k_tail, bias_t), state)
    out_s = [acc * (1.0 / jnp.maximum(l_run, TINY)) for _, l_run, acc in state]

    gt = gt_ref[0].T
    total = []
    for g in range(N_GROUPS):
        acc = None
        for br, outs in enumerate((out_c, out_s, out_w)):
            rows = [3 * (g * GROUP_SIZE + j) + br for j in range(GROUP_SIZE)]
            gate = jnp.concatenate([gt[r:r + 1, :] for r in rows], axis=1)
            term = gate * outs[g]
            acc = term if acc is None else acc + term
        total.append(acc)
    merged = jnp.where(row_id(LANES) < HEAD_DIM, total[0], total[1])
    o_ref[0] = jnp.concatenate(
        [merged[:, j * Q_BLOCK:(j + 1) * Q_BLOCK].T for j in range(GROUP_SIZE)], axis=1).astype(BF16)


def _nsa_attention(q, gates, kc, vct, ksa, ksb, vst, kw, vwt, *, ck):
    b, t, _ = q.shape
    n_cmp_pad = kc.shape[1]
    n_sb = t // SEL_BLOCK
    assert Q_BLOCK == LANES
    assert N_FORCED <= min(N_SELECT, n_sb) and n_sb <= HEAD_DIM, "selection blocks must fit the 64 spare lanes"
    ck = min(ck, t)
    ci = np.arange(n_cmp_pad)[None, :] * CMP_STRIDE
    sj = np.arange(n_sb)[:, None] * SEL_BLOCK
    ov = (ci < sj + SEL_BLOCK) & (ci + CMP_BLOCK - 1 >= sj)
    overlap = np.zeros((N_GROUPS, LANES, n_cmp_pad), np.float32)
    overlap[0, HEAD_DIM:HEAD_DIM + n_sb] = ov
    overlap[1, 0:n_sb] = ov
    overlap = jnp.asarray(overlap, BF16)

    tok_spec = pl.BlockSpec((1, t, KV_WIDTH), lambda i, j: (i, 0, 0))
    feat_spec = pl.BlockSpec((1, KV_WIDTH, t), lambda i, j: (i, 0, 0))
    return pl.pallas_call(
        functools.partial(_nsa_kernel, t_len=t, ck=ck),
        grid=(b, t // Q_BLOCK),
        in_specs=[
            pl.BlockSpec((1, Q_BLOCK, NSA_WIDTH), lambda i, j: (i, j, 0)),
            pl.BlockSpec((1, Q_BLOCK, GATE_PAD), lambda i, j: (i, j, 0)),
            pl.BlockSpec((1, n_cmp_pad, KV_WIDTH), lambda i, j: (i, 0, 0)),
            pl.BlockSpec((1, KV_WIDTH, n_cmp_pad), lambda i, j: (i, 0, 0)),
            tok_spec, tok_spec, feat_spec, tok_spec, feat_spec,
            _const_spec((N_GROUPS, LANES, n_cmp_pad)),
        ],
        out_specs=pl.BlockSpec((1, Q_BLOCK, NSA_WIDTH), lambda i, j: (i, j, 0)),
        out_shape=jax.ShapeDtypeStruct((b, t, NSA_WIDTH), BF16),
        scratch_shapes=[pltpu.VMEM((N_GROUPS, ck, GROUP_SIZE * Q_BLOCK), F32)] * 2
        + [pltpu.VMEM((N_GROUPS, min(WINDOW + Q_BLOCK, t), GROUP_SIZE * Q_BLOCK), F32)],
        compiler_params=pltpu.CompilerParams(
            dimension_semantics=("arbitrary", "arbitrary"), vmem_limit_bytes=VMEM_LIMIT),
        name="nsa_attention",
    )(q, gates, kc, vct, ksa, ksb, vst, kw, vwt, overlap)


def _tile(t, want):
    return want if t % want == 0 else t


def kernel(x, norm_mix, norm_ffn, ffn_gate, ffn_up, ffn_down, w_in_even, conv_a, pool_w, pool_scale,
           w_out_even, w_in_odd, q_norm, k_norm, cmp_pos, cmp_w1, cmp_w2, conf_dw, conf_dw_b,
           conf_ln_g, conf_ln_b, w_out_odd):
    depth = norm_mix.shape[0]
    t = x.shape[1]
    tm = _tile(t, 512)
    d_ff = ffn_gate.shape[2]
    chunk = 256 if d_ff % 256 == 0 else d_ff
    for i in range(depth):
        if i % 2 == 0:
            e = i // 2
            a, z = _even_mixer(x, norm_mix[i], w_in_even[e], conv_a[e], pool_w[e], pool_scale[e], tm=tm)
            w_out = w_out_even[e]
        else:
            o = i // 2
            q, kc_tok, vc_tok, ksa, ksb, vs, kw, vw, gates, z = _odd_proj(
                x, norm_mix[i], w_in_odd[o], q_norm[o], k_norm[o], conf_dw[o], conf_dw_b[o],
                conf_ln_g[o], conf_ln_b[o], tm=tm)
            kc, vc = _compress(kc_tok, vc_tok, cmp_pos[o], cmp_w1[o], cmp_w2[o], k_norm[o, 2])
            a = _nsa_attention(q, gates, kc, vc, ksa, ksb, vs, kw, vw, ck=512)
            w_out = jnp.concatenate([w_out_odd[o][:NSA_WIDTH][_q_perm()], w_out_odd[o][NSA_WIDTH:]], axis=0)
        x = _ffn(x, a, z, w_out, norm_ffn[i], ffn_gate[i], ffn_up[i], ffn_down[i], tm=tm, chunk=chunk)
    return x
```

```python
import functools

import numpy as np
import jax
import jax.numpy as jnp
from jax import lax
from jax.experimental import pallas as pl
from jax.experimental.pallas import tpu as pltpu

EPS = 1e-6
ROPE_THETA = 10000.0
HEAD_DIM = 64
HALF = HEAD_DIM // 2
LANES = 128
SUBLANES = 8
LOG2E = 1.4426950408889634
N_HEADS = 8
N_GROUPS = 2
GROUP_SIZE = N_HEADS // N_GROUPS
NSA_WIDTH = N_HEADS * HEAD_DIM
KV_WIDTH = N_GROUPS * HEAD_DIM
CONF_WIDTH = 512
CONF_K = 31
CONV_WIDTH = 512
POOL_WINDOWS = (2, 4, 8, 16)
POOL_GROUP = 128
SHORT_CONV_K = 3
CMP_BLOCK = 32
CMP_STRIDE = 16
CMP_HIDDEN = 4 * HEAD_DIM
SEL_BLOCK = 64
N_SELECT = 8
N_FORCED = 3
WINDOW = 512
Q_BLOCK = 128
NEG = -1e30
M_FLOOR = -1e29
TINY = 1e-30
GATE_PAD = LANES
ODD_COLS = NSA_WIDTH + 6 * KV_WIDTH + GATE_PAD + 2 * CONF_WIDTH

VMEM_LIMIT = 56 * 1024 * 1024
N_SUB = 2

BF16 = jnp.bfloat16
F32 = jnp.float32


def _dot(a, b):
    return jnp.dot(a, b, preferred_element_type=F32)


def _dot_nt(a, b):
    return lax.dot_general(a, b, (((1,), (1,)), ((), ())), preferred_element_type=F32)


def _split_dot(a, b):
    hi = a.astype(BF16)
    lo = (a - hi.astype(F32)).astype(BF16)
    return _dot(hi, b) + _dot(lo, b)


def _rms_rows(x, g):
    ms = jnp.mean(x * x, axis=-1, keepdims=True)
    return x * lax.rsqrt(ms + EPS) * g


def _sigmoid(x):
    return 1.0 / (1.0 + jnp.exp(-x))


def _const_spec(shape):
    nd = len(shape)
    return pl.BlockSpec(shape, lambda *_: (0,) * nd, pipeline_mode=pl.Buffered(1))


def _ffn_kernel(x_ref, a_ref, z_ref, wa_ref, wz_ref, g_ref, wg_ref, wu_ref, wd_ref, o_ref, x1_buf, *, chunk):
    x1_buf[...] = x_ref[0] + (_dot(a_ref[0], wa_ref[...]) + _dot(z_ref[0], wz_ref[...]))
    x = x1_buf[...]
    h = _rms_rows(x, g_ref[...]).astype(BF16)
    acc = x
    for c0 in range(0, wg_ref.shape[1], chunk):
        gate = _dot(h, wg_ref[:, c0:c0 + chunk])
        up = _dot(h, wu_ref[:, c0:c0 + chunk])
        a = (gate * _sigmoid(gate) * up).astype(BF16)
        acc = acc + _dot(a, wd_ref[c0:c0 + chunk, :])
    o_ref[0] = acc


def _ffn(x, a, z, w_out, g, wg, wu, wd, *, tm, chunk):
    b, t, d = x.shape
    wa_n, wz_n = a.shape[2], z.shape[2]
    dff = wg.shape[1]
    return pl.pallas_call(
        functools.partial(_ffn_kernel, chunk=chunk),
        grid=(b, t // tm),
        in_specs=[
            pl.BlockSpec((1, tm, d), lambda i, j: (i, j, 0)),
            pl.BlockSpec((1, tm, wa_n), lambda i, j: (i, j, 0)),
            pl.BlockSpec((1, tm, wz_n), lambda i, j: (i, j, 0)),
            _const_spec((wa_n, d)),
            _const_spec((wz_n, d)),
            _const_spec((1, d)),
            _const_spec((d, dff)),
            _const_spec((d, dff)),
            _const_spec((dff, d)),
        ],
        out_specs=pl.BlockSpec((1, tm, d), lambda i, j: (i, j, 0)),
        out_shape=jax.ShapeDtypeStruct(x.shape, F32),
        scratch_shapes=[pltpu.VMEM((tm, d), F32)],
        compiler_params=pltpu.CompilerParams(
            dimension_semantics=("arbitrary", "arbitrary"), vmem_limit_bytes=VMEM_LIMIT),
        name="ffn",
    )(x, a, z, w_out[:wa_n].astype(BF16), w_out[wa_n:].astype(BF16), g.reshape(1, d),
      wg.astype(BF16), wu.astype(BF16), wd.astype(BF16))


def _even_kernel(x_ref, g_ref, win_ref, cw_ref, pw_ref, ps_ref, yc_ref, yp_ref,
                 cv_buf, vp_buf, u0_buf, u1_buf, *, tm):
    t = pl.program_id(1)
    half = tm // N_SUB

    @pl.when(t == 0)
    def _():
        cv_buf[0:8, :] = jnp.zeros((8, CONV_WIDTH), F32)
        vp_buf[0:16, :] = jnp.zeros((16, 4 * POOL_GROUP), F32)

    @pl.when(t > 0)
    def _():
        cv_buf[0:8, :] = cv_buf[tm:tm + 8, :]
        vp_buf[0:16, :] = vp_buf[tm:tm + 16, :]

    cw = cw_ref[...]
    for hh, ub in enumerate((u0_buf, u1_buf)):
        r0 = hh * half
        xs = x_ref[0, r0:r0 + half, :]
        ub[...] = _dot(_rms_rows(xs, g_ref[...]).astype(BF16), win_ref[...])
        b_gate = ub[:, 0:CONV_WIDTH]
        cv = ub[:, CONV_WIDTH:2 * CONV_WIDTH] * ub[:, 2 * CONV_WIDTH:3 * CONV_WIDTH]
        vp = ub[:, 3 * CONV_WIDTH:]
        cv_buf[8 + r0:8 + r0 + half, :] = cv
        vp_buf[16 + r0:16 + r0 + half, :] = vp

        conv = cw[2:3, :] * cv
        for k in range(SHORT_CONV_K - 1):
            off = r0 + 8 - (SHORT_CONV_K - 1) + k
            conv = conv + cw[k:k + 1, :] * cv_buf[off:off + half, :]
        yc_ref[0, r0:r0 + half, :] = (b_gate * conv).astype(BF16)

        pos = t * tm + r0 + lax.broadcasted_iota(jnp.int32, (half, 1), 0)
        ys = []
        for gi, w in enumerate(POOL_WINDOWS):
            lo, hi = gi * POOL_GROUP, (gi + 1) * POOL_GROUP
            s = vp[:, lo:hi]
            for k in range(1, w):
                s = s + vp_buf[r0 + 16 - k:r0 + 16 - k + half, lo:hi]
            cnt = jnp.minimum(pos + 1, w).astype(F32)
            pooled = s / cnt - vp[:, lo:hi]
            ys.append(_dot(pooled.astype(BF16), pw_ref[gi]))
        yp_ref[0, r0:r0 + half, :] = (jnp.concatenate(ys, axis=-1) * ps_ref[...]).astype(BF16)


def _even_mixer(x, g, w_in, conv_w, pool_w, pool_scale, *, tm):
    b, t, d = x.shape
    n_in = w_in.shape[1]
    return pl.pallas_call(
        functools.partial(_even_kernel, tm=tm),
        grid=(b, t // tm),
        in_specs=[
            pl.BlockSpec((1, tm, d), lambda i, j: (i, j, 0)),
            _const_spec((1, d)),
            _const_spec((d, n_in)),
            _const_spec((SHORT_CONV_K, CONV_WIDTH)),
            _const_spec((len(POOL_WINDOWS), POOL_GROUP, POOL_GROUP)),
            _const_spec((1, 4 * POOL_GROUP)),
        ],
        out_specs=[pl.BlockSpec((1, tm, CONV_WIDTH), lambda i, j: (i, j, 0)),
                   pl.BlockSpec((1, tm, 4 * POOL_GROUP), lambda i, j: (i, j, 0))],
        out_shape=[jax.ShapeDtypeStruct((b, t, CONV_WIDTH), BF16),
                   jax.ShapeDtypeStruct((b, t, 4 * POOL_GROUP), BF16)],
        scratch_shapes=[pltpu.VMEM((tm + 8, CONV_WIDTH), F32),
                        pltpu.VMEM((tm + 16, 4 * POOL_GROUP), F32)]
        + [pltpu.VMEM((tm // N_SUB, n_in), F32)] * N_SUB,
        compiler_params=pltpu.CompilerParams(
            dimension_semantics=("arbitrary", "arbitrary"), vmem_limit_bytes=VMEM_LIMIT),
        name="even_mixer",
    )(x, g.reshape(1, d), w_in.astype(BF16), conv_w, pool_w.astype(BF16), pool_scale.reshape(1, -1))


def _rope_tile(x, cos, sin_signed, first_half):
    rot = jnp.where(first_half, pltpu.roll(x, LANES - HALF, 1), pltpu.roll(x, HALF, 1))
    return x * cos + rot * sin_signed


def _head_ms(x, ones_bd):
    return _split_dot(x * x, ones_bd) * (1.0 / HEAD_DIM)


def _odd_proj_kernel(x_ref, g_ref, win_ref, cos_ref, sin_ref, qn_ref, kn_ref, bd_ref,
                     dw_ref, dwb_ref, lng_ref, lnb_ref,
                     q_ref, kc_ref, vc_ref, ksa_ref, ksb_ref, vs_ref, kw_ref, vw_ref, gt_ref, z_ref,
                     z_buf, ph_buf, u0_buf, u1_buf, *, tm):
    t = pl.program_id(1)
    half = tm // N_SUB
    halo = 32
    lane = lax.broadcasted_iota(jnp.int32, (1, LANES), 1)
    first_half = (lane % HEAD_DIM) < HALF
    low = lane < HEAD_DIM
    bd = bd_ref[...]
    dw = dw_ref[...]
    scale = LOG2E * HEAD_DIM ** -0.5

    @pl.when(t == 0)
    def _():
        z_buf[0:halo, :] = jnp.zeros((halo, CONF_WIDTH), F32)

    @pl.when(t > 0)
    def _():
        z_buf[0:halo, :] = z_buf[tm:tm + halo, :]

    for hh, u in enumerate((u0_buf, u1_buf)):
        r0 = hh * half
        rows = slice(r0, r0 + half)
        xs = x_ref[0, rows, :]
        u[...] = _dot(_rms_rows(xs, g_ref[...]).astype(BF16), win_ref[...])
        cos = cos_ref[rows, :]
        sin = sin_ref[rows, :]

        q_parts = []
        for j in range(NSA_WIDTH // LANES):
            qj = u[:, j * LANES:(j + 1) * LANES]
            qj = qj * lax.rsqrt(_head_ms(qj, bd) + EPS) * qn_ref[...]
            q_parts.append(_rope_tile(qj, cos, sin, first_half) * scale)
        q_ref[0, rows, :] = jnp.concatenate(q_parts, axis=-1).astype(BF16)

        o = NSA_WIDTH
        kc_ref[0, rows, :] = _rope_tile(u[:, o:o + KV_WIDTH], cos, sin, first_half).astype(BF16)
        o += KV_WIDTH
        vc_ref[0, rows, :] = u[:, o:o + KV_WIDTH].astype(BF16)
        o += KV_WIDTH
        k = u[:, o:o + KV_WIDTH]
        k = k * lax.rsqrt(_head_ms(k, bd) + EPS) * kn_ref[0:1, :]
        k = _rope_tile(k, cos, sin, first_half)
        blk = (t * tm + r0 + lax.broadcasted_iota(jnp.int32, (half, 1), 0)) // SEL_BLOCK
        ksa_ref[0, rows, :] = jnp.where(low, k, jnp.where(lane - HEAD_DIM == blk, NEG, 0.0)).astype(BF16)
        ksb_ref[0, rows, :] = jnp.where(low, jnp.where(lane == blk, NEG, 0.0), k).astype(BF16)
        o += KV_WIDTH
        vs_ref[0, :, rows] = u[:, o:o + KV_WIDTH].T.astype(BF16)
        o += KV_WIDTH
        k = u[:, o:o + KV_WIDTH]
        k = k * lax.rsqrt(_head_ms(k, bd) + EPS) * kn_ref[1:2, :]
        kw_ref[0, rows, :] = _rope_tile(k, cos, sin, first_half).astype(BF16)
        o += KV_WIDTH
        vw_ref[0, :, rows] = u[:, o:o + KV_WIDTH].T.astype(BF16)
        o += KV_WIDTH
        gt_ref[0, rows, :] = _sigmoid(u[:, o:o + GATE_PAD])
        o += GATE_PAD

        zz = u[:, o:o + CONF_WIDTH] * _sigmoid(u[:, o + CONF_WIDTH:o + 2 * CONF_WIDTH])
        z_buf[halo + r0:halo + r0 + half, :] = zz
        for r in range(1, SUBLANES):
            ph_buf[r - 1] = z_buf[r0 + r:r0 + r + half + halo - SUBLANES, :]
        conv = dw[CONF_K - 1:CONF_K, :] * zz + dwb_ref[...]
        for kk in range(CONF_K - 1):
            off = halo - (CONF_K - 1) + kk
            a, r = off // SUBLANES * SUBLANES, off % SUBLANES
            tap = z_buf[r0 + a:r0 + a + half, :] if r == 0 else ph_buf[r - 1, a:a + half, :]
            conv = conv + dw[kk:kk + 1, :] * tap
        mu = jnp.mean(conv, axis=-1, keepdims=True)
        cen = conv - mu
        var = jnp.mean(cen * cen, axis=-1, keepdims=True)
        y = cen * lax.rsqrt(var + EPS) * lng_ref[...] + lnb_ref[...]
        z_ref[0, rows, :] = (y * _sigmoid(y)).astype(BF16)


def _q_perm():
    p = np.arange(NSA_WIDTH)
    j, within = p // LANES, p % LANES
    head = np.where(within < HEAD_DIM, j, GROUP_SIZE + j)
    return head * HEAD_DIM + within % HEAD_DIM


def _odd_proj(x, g, w_in, q_norm, k_norm, conf_dw, conf_dw_b, conf_ln_g, conf_ln_b, *, tm):
    b, t, d = x.shape
    perm = _q_perm()
    n_kv = 6 * KV_WIDTH
    n_gate = 3 * N_HEADS
    w_q = w_in[:, :NSA_WIDTH][:, perm]
    w_kv = w_in[:, NSA_WIDTH:NSA_WIDTH + n_kv]
    w_gt = jnp.pad(w_in[:, NSA_WIDTH + n_kv:NSA_WIDTH + n_kv + n_gate], ((0, 0), (0, GATE_PAD - n_gate)))
    w_cf = w_in[:, NSA_WIDTH + n_kv + n_gate:]
    w_all = jnp.concatenate([w_q, w_kv, w_gt, w_cf], axis=1).astype(BF16)

    inv = 1.0 / (ROPE_THETA ** (jnp.arange(0, HEAD_DIM, 2, dtype=F32) / HEAD_DIM))
    ang = jnp.arange(t, dtype=F32)[:, None] * inv[None, :]
    cos = jnp.tile(jnp.cos(ang), (1, LANES // HALF))
    sin = jnp.tile(jnp.sin(ang), (1, LANES // HALF))
    lane = np.arange(LANES)
    sin_signed = jnp.where((lane % HEAD_DIM < HALF)[None, :], -sin, sin)

    ones_bd = jnp.asarray((lane[:, None] // HEAD_DIM == lane[None, :] // HEAD_DIM), BF16)
    qn = jnp.tile(q_norm, LANES // HEAD_DIM).reshape(1, LANES)
    kn = jnp.tile(k_norm[:2], (1, LANES // HEAD_DIM))

    tok = lambda w, dt: jax.ShapeDtypeStruct((b, t, w), dt)
    tok_spec = lambda w: pl.BlockSpec((1, tm, w), lambda i, j: (i, j, 0))
    feat = jax.ShapeDtypeStruct((b, KV_WIDTH, t), BF16)
    feat_spec = pl.BlockSpec((1, KV_WIDTH, tm), lambda i, j: (i, 0, j))
    return pl.pallas_call(
        functools.partial(_odd_proj_kernel, tm=tm),
        grid=(b, t // tm),
        in_specs=[
            tok_spec(d),
            _const_spec((1, d)),
            _const_spec((d, ODD_COLS)),
            pl.BlockSpec((tm, LANES), lambda i, j: (j, 0)),
            pl.BlockSpec((tm, LANES), lambda i, j: (j, 0)),
            _const_spec((1, LANES)),
            _const_spec((2, LANES)),
            _const_spec((LANES, LANES)),
            _const_spec((CONF_K, CONF_WIDTH)),
            _const_spec((1, CONF_WIDTH)),
            _const_spec((1, CONF_WIDTH)),
            _const_spec((1, CONF_WIDTH)),
        ],
        out_specs=([tok_spec(NSA_WIDTH)] + [tok_spec(KV_WIDTH)] * 4 + [feat_spec, tok_spec(KV_WIDTH), feat_spec]
                   + [tok_spec(GATE_PAD), tok_spec(CONF_WIDTH)]),
        out_shape=([tok(NSA_WIDTH, BF16)] + [tok(KV_WIDTH, BF16)] * 4 + [feat, tok(KV_WIDTH, BF16), feat]
                   + [tok(GATE_PAD, F32), tok(CONF_WIDTH, BF16)]),
        scratch_shapes=[pltpu.VMEM((tm + 32, CONF_WIDTH), F32),
                        pltpu.VMEM((SUBLANES - 1, tm // N_SUB + 32 - SUBLANES, CONF_WIDTH), F32)]
        + [pltpu.VMEM((tm // N_SUB, ODD_COLS), F32)] * N_SUB,
        compiler_params=pltpu.CompilerParams(
            dimension_semantics=("arbitrary", "arbitrary"), vmem_limit_bytes=VMEM_LIMIT),
        name="odd_proj",
    )(x, g.reshape(1, d), w_all, cos, sin_signed, qn, kn, ones_bd,
      conf_dw, conf_dw_b.reshape(1, -1), conf_ln_g.reshape(1, -1), conf_ln_b.reshape(1, -1))


def _gelu_tanh(x):
    return 0.5 * x * (1.0 + jnp.tanh(np.sqrt(2.0 / np.pi) * (x + 0.044715 * (x * x * x))))


def _compress_kernel(kt_ref, vt_ref, w1_ref, pos_ref, w2_ref, kn_ref, kc_ref, vc_ref, sh_buf, *, n_rows):
    hw = N_GROUPS * CMP_HIDDEN
    outs = []
    for s, tok_ref in enumerate((kt_ref, vt_ref)):
        rows = tok_ref[0]
        first = _dot(rows, w1_ref[s, 0])
        second = _dot(rows, w1_ref[s, 1])
        sh_buf[0:n_rows, :] = second
        sh_buf[n_rows:n_rows + 8, :] = jnp.zeros((8, hw), F32)
        bias = _dot(pos_ref[s, 0], w1_ref[s, 0]) + _dot(pos_ref[s, 1], w1_ref[s, 1])
        hid = first + sh_buf[1:n_rows + 1, :] + bias[0:1, :]
        act = _gelu_tanh(hid).astype(BF16)
        per_group = [_dot(act[:, gi * CMP_HIDDEN:(gi + 1) * CMP_HIDDEN], w2_ref[s]) for gi in range(N_GROUPS)]
        outs.append(per_group)
    kc = [_rms_rows(o, kn_ref[...]) for o in outs[0]]
    kc_ref[0] = jnp.concatenate(kc, axis=-1).astype(BF16)
    vc_ref[0] = jnp.concatenate(outs[1], axis=-1).T.astype(BF16)


def _compress(kc_tok, vc_tok, cmp_pos, cmp_w1, cmp_w2, k_norm2):
    b, t, _ = kc_tok.shape
    n_rows = t // CMP_STRIDE
    row_w = CMP_STRIDE * KV_WIDTH
    hw = N_GROUPS * CMP_HIDDEN
    w1 = cmp_w1.reshape(2, 2, CMP_STRIDE, HEAD_DIM, CMP_HIDDEN)
    eye = jnp.eye(N_GROUPS, dtype=cmp_w1.dtype)
    w1x = jnp.einsum('shrdc,gk->shrgdkc', w1, eye).reshape(2, 2, row_w, hw).astype(BF16)
    pos = cmp_pos.reshape(2, 2, CMP_STRIDE, 1, HEAD_DIM)
    posx = jnp.broadcast_to(pos, (2, 2, CMP_STRIDE, N_GROUPS, HEAD_DIM)).reshape(2, 2, 1, row_w)
    posx = jnp.broadcast_to(posx, (2, 2, 8, row_w)).astype(BF16)
    out = jax.ShapeDtypeStruct((b, n_rows, KV_WIDTH), BF16)
    return pl.pallas_call(
        functools.partial(_compress_kernel, n_rows=n_rows),
        grid=(b,),
        in_specs=[
            pl.BlockSpec((1, n_rows, row_w), lambda i: (i, 0, 0)),
            pl.BlockSpec((1, n_rows, row_w), lambda i: (i, 0, 0)),
            _const_spec((2, 2, row_w, hw)),
            _const_spec((2, 2, 8, row_w)),
            _const_spec((2, CMP_HIDDEN, HEAD_DIM)),
            _const_spec((1, HEAD_DIM)),
        ],
        out_specs=[pl.BlockSpec((1, n_rows, KV_WIDTH), lambda i: (i, 0, 0)),
                   pl.BlockSpec((1, KV_WIDTH, n_rows), lambda i: (i, 0, 0))],
        out_shape=[out, jax.ShapeDtypeStruct((b, KV_WIDTH, n_rows), BF16)],
        scratch_shapes=[pltpu.VMEM((n_rows + 8, hw), F32)],
        compiler_params=pltpu.CompilerParams(
            dimension_semantics=("arbitrary",), vmem_limit_bytes=VMEM_LIMIT),
        name="compress",
    )(kc_tok.reshape(b, n_rows, row_w), vc_tok.reshape(b, n_rows, row_w), w1x, posx,
      cmp_w2.astype(BF16), k_norm2.reshape(1, HEAD_DIM))


def _per_head(bias):
    return jnp.concatenate([bias] * GROUP_SIZE, axis=1)


def _exp_keys(s):
    m = jnp.maximum(jnp.max(s, axis=0, keepdims=True), M_FLOOR)
    e = jnp.exp2(s - m)
    return e, jnp.sum(e, axis=0, keepdims=True)


def _nsa_kernel(q_ref, gt_ref, kc_ref, vct_ref, ksa_ref, ksb_ref, vst_ref, kw_ref, vwt_ref, ovt_ref, o_ref,
                s_a, s_b, s_w, *, t_len, ck):
    qb = pl.program_id(1)
    q0 = qb * Q_BLOCK
    cols = GROUP_SIZE * Q_BLOCK
    n_cmp_pad = kc_ref.shape[1]
    n_sb = t_len // SEL_BLOCK
    n_sel = min(N_SELECT, n_sb)
    win_len = min(WINDOW + Q_BLOCK, t_len)

    q_all = q_ref[0]
    lane = lax.broadcasted_iota(jnp.int32, (1, LANES), 1)
    low = lane < HEAD_DIM
    pos_q = q0 + lane
    row_id = lambda n: lax.broadcasted_iota(jnp.int32, (n, 1), 0)

    cmp_end = row_id(n_cmp_pad) * CMP_STRIDE + (CMP_BLOCK - 1)
    bias_c = _per_head(jnp.where(cmp_end <= pos_q, 0.0, NEG))
    w0 = pl.multiple_of(jnp.maximum(q0 + Q_BLOCK - win_len, 0), Q_BLOCK)
    kpos_w = w0 + row_id(win_len)
    bias_w = _per_head(jnp.where((kpos_w <= pos_q) & (kpos_w > pos_q - WINDOW), 0.0, NEG))
    n_full = q0 // ck
    k_tail = pl.multiple_of(n_full * ck, ck)
    bias_t = _per_head(jnp.where(k_tail + row_id(ck) <= pos_q, 0.0, NEG))

    ks_refs = (ksa_ref, ksb_ref)
    q_tiles = [q_all[:, j * LANES:(j + 1) * LANES] for j in range(GROUP_SIZE)]
    is_q = (low, jnp.logical_not(low))
    qz = [jnp.concatenate([jnp.where(is_q[g], qt, jnp.zeros((), BF16)) for qt in q_tiles], axis=0)
          for g in range(N_GROUPS)]
    for g in range(N_GROUPS):
        s_w[g] = _dot_nt(kw_ref[0, pl.ds(w0, win_len), :], qz[g])
    q_sel = []
    out_c, out_w = [], []
    for g in range(N_GROUPS):

        e_c, l_c = _exp_keys(_dot_nt(kc_ref[0], qz[g]) + bias_c)
        p_c = e_c * (1.0 / jnp.maximum(l_c, TINY))
        feats = slice(g * HEAD_DIM, (g + 1) * HEAD_DIM)
        out_c.append(_dot(vct_ref[0, feats, :], p_c.astype(BF16)))

        p_sum = p_c[:, 0:Q_BLOCK]
        for j in range(1, GROUP_SIZE):
            p_sum = p_sum + p_c[:, j * Q_BLOCK:(j + 1) * Q_BLOCK]
        hi = p_sum.astype(BF16)
        lo = (p_sum - hi.astype(F32)).astype(BF16)
        imp = _dot(ovt_ref[g], hi) + _dot(ovt_ref[g], lo)
        jj = row_id(LANES) - (HEAD_DIM if g == 0 else 0)
        in_blk = (jj >= 0) & (jj < n_sb)
        cur = pos_q // SEL_BLOCK
        forced = in_blk & ((jj == 0) | (jj == cur) | (jj == cur - 1))
        score = jnp.where(jj * SEL_BLOCK <= pos_q, imp, -1.0)
        score = jnp.where(in_blk & jnp.logical_not(forced), score, -jnp.inf)
        not_sel = jnp.where(forced, 0.0, 1.0)
        jjf = jj.astype(F32)
        for _ in range(n_sel - N_FORCED):
            m = jnp.max(score, axis=0, keepdims=True)
            first = jnp.min(jnp.where(score == m, jjf, float(LANES)), axis=0, keepdims=True)
            pick = jjf == first
            not_sel = jnp.where(pick, 0.0, not_sel)
            score = jnp.where(pick, -jnp.inf, score)
        not_sel = not_sel.T.astype(BF16)
        q_sel.append(jnp.concatenate([jnp.where(is_q[g], qt, not_sel) for qt in q_tiles], axis=0))

        e_w, l_w = _exp_keys(s_w[g] + bias_w)
        o_w = _dot(vwt_ref[0, feats, pl.ds(w0, win_len)], e_w.astype(BF16))
        out_w.append(o_w * (1.0 / jnp.maximum(l_w, TINY)))

    def sel_scores(buf, k0):
        for g in range(N_GROUPS):
            buf[g] = _dot_nt(ks_refs[g][0, pl.ds(k0, ck), :], q_sel[g])

    def sel_update(state, buf, k0, bias=None):
        new = []
        for g in range(N_GROUPS):
            m_run, l_run, acc = state[g]
            s = buf[g] if bias is None else buf[g] + bias
            m_new = jnp.maximum(m_run, jnp.max(s, axis=0, keepdims=True))
            alpha = jnp.exp2(m_run - m_new)
            e = jnp.exp2(s - m_new)
            l_new = alpha * l_run + jnp.sum(e, axis=0, keepdims=True)
            v_g = vst_ref[0, g * HEAD_DIM:(g + 1) * HEAD_DIM, pl.ds(k0, ck)]
            acc = alpha * acc + _dot(v_g, e.astype(BF16))
            new.append((m_new, l_new, acc))
        return tuple(new)

    def pair_body(i, state):
        k0 = pl.multiple_of(2 * i * ck, ck)
        sel_scores(s_b, k0 + ck)
        state = sel_update(state, s_a, k0)
        sel_scores(s_a, k0 + 2 * ck)
        return sel_update(state, s_b, k0 + ck)

    one = (jnp.full((1, cols), M_FLOOR, F32), jnp.zeros((1, cols), F32), jnp.zeros((HEAD_DIM, cols), F32))
    sel_scores(s_a, 0)
    state = lax.fori_loop(0, n_full // 2, pair_body, (one,) * N_GROUPS)

    def odd_rest(st):
        sel_scores(s_b, k_tail)
        st = sel_update(st, s_a, k_tail - ck)
        return sel_update(st, s_b, k_tail, bias_t)

    state = lax.cond(n_full % 2 == 1, odd_rest, lambda st: sel_update(st, s_a, k_tail, bias_t), state)
    out_s = [acc * (1.0 / jnp.maximum(l_run, TINY)) for _, l_run, acc in state]

    gt = gt_ref[0].T
    total = []
    for g in range(N_GROUPS):
        acc = None
        for br, outs in enumerate((out_c, out_s, out_w)):
            rows = [3 * (g * GROUP_SIZE + j) + br for j in range(GROUP_SIZE)]
            gate = jnp.concatenate([gt[r:r + 1, :] for r in rows], axis=1)
            term = gate * outs[g]
            acc = term if acc is None else acc + term
        total.append(acc)
    merged = jnp.concatenate(total, axis=0)
    o_ref[0] = jnp.concatenate(
        [merged[:, j * Q_BLOCK:(j + 1) * Q_BLOCK].T for j in range(GROUP_SIZE)], axis=1).astype(BF16)


def _nsa_attention(q, gates, kc, vct, ksa, ksb, vst, kw, vwt, *, ck):
    b, t, _ = q.shape
    n_cmp_pad = kc.shape[1]
    n_sb = t // SEL_BLOCK
    assert Q_BLOCK == LANES
    assert N_FORCED <= min(N_SELECT, n_sb) and n_sb <= HEAD_DIM, "selection blocks must fit the 64 spare lanes"
    ck = min(ck, t)
    ci = np.arange(n_cmp_pad)[None, :] * CMP_STRIDE
    sj = np.arange(n_sb)[:, None] * SEL_BLOCK
    ov = (ci < sj + SEL_BLOCK) & (ci + CMP_BLOCK - 1 >= sj)
    overlap = np.zeros((N_GROUPS, LANES, n_cmp_pad), np.float32)
    overlap[0, HEAD_DIM:HEAD_DIM + n_sb] = ov
    overlap[1, 0:n_sb] = ov
    overlap = jnp.asarray(overlap, BF16)

    tok_spec = pl.BlockSpec((1, t, KV_WIDTH), lambda i, j: (i, 0, 0))
    feat_spec = pl.BlockSpec((1, KV_WIDTH, t), lambda i, j: (i, 0, 0))
    return pl.pallas_call(
        functools.partial(_nsa_kernel, t_len=t, ck=ck),
        grid=(b, t // Q_BLOCK),
        in_specs=[
            pl.BlockSpec((1, Q_BLOCK, NSA_WIDTH), lambda i, j: (i, j, 0)),
            pl.BlockSpec((1, Q_BLOCK, GATE_PAD), lambda i, j: (i, j, 0)),
            pl.BlockSpec((1, n_cmp_pad, KV_WIDTH), lambda i, j: (i, 0, 0)),
            pl.BlockSpec((1, KV_WIDTH, n_cmp_pad), lambda i, j: (i, 0, 0)),
            tok_spec, tok_spec, feat_spec, tok_spec, feat_spec,
            _const_spec((N_GROUPS, LANES, n_cmp_pad)),
        ],
        out_specs=pl.BlockSpec((1, Q_BLOCK, NSA_WIDTH), lambda i, j: (i, j, 0)),
        out_shape=jax.ShapeDtypeStruct((b, t, NSA_WIDTH), BF16),
        scratch_shapes=[pltpu.VMEM((N_GROUPS, ck, GROUP_SIZE * Q_BLOCK), F32)] * 2
        + [pltpu.VMEM((N_GROUPS, min(WINDOW + Q_BLOCK, t), GROUP_SIZE * Q_BLOCK), F32)],
        compiler_params=pltpu.CompilerParams(
            dimension_semantics=("arbitrary", "arbitrary"), vmem_limit_bytes=VMEM_LIMIT),
        name="nsa_attention",
    )(q, gates, kc, vct, ksa, ksb, vst, kw, vwt, overlap)


def _tile(t, want):
    return want if t % want == 0 else t


def kernel(x, norm_mix, norm_ffn, ffn_gate, ffn_up, ffn_down, w_in_even, conv_a, pool_w, pool_scale,
           w_out_even, w_in_odd, q_norm, k_norm, cmp_pos, cmp_w1, cmp_w2, conf_dw, conf_dw_b,
           conf_ln_g, conf_ln_b, w_out_odd):
    depth = norm_mix.shape[0]
    t = x.shape[1]
    tm = _tile(t, 512)
    tm_mix = _tile(t, N_SUB * 512)
    d_ff = ffn_gate.shape[2]
    chunk = 256 if d_ff % 256 == 0 else d_ff
    for i in range(depth):
        if i % 2 == 0:
            e = i // 2
            a, z = _even_mixer(x, norm_mix[i], w_in_even[e], conv_a[e], pool_w[e], pool_scale[e], tm=tm_mix)
            w_out = w_out_even[e]
        else:
            o = i // 2
            q, kc_tok, vc_tok, ksa, ksb, vs, kw, vw, gates, z = _odd_proj(
                x, norm_mix[i], w_in_odd[o], q_norm[o], k_norm[o], conf_dw[o], conf_dw_b[o],
                conf_ln_g[o], conf_ln_b[o], tm=tm_mix)
            kc, vc = _compress(kc_tok, vc_tok, cmp_pos[o], cmp_w1[o], cmp_w2[o], k_norm[o, 2])
            a = _nsa_attention(q, gates, kc, vc, ksa, ksb, vs, kw, vw, ck=512)
            w_out = jnp.concatenate([w_out_odd[o][:NSA_WIDTH][_q_perm()], w_out_odd[o][NSA_WIDTH:]], axis=0)
        x = _ffn(x, a, z, w_out, norm_ffn[i], ffn_gate[i], ffn_up[i], ffn_down[i], tm=tm, chunk=chunk)
    return x
```
